```python
import math
import jax
import jax.numpy as jnp
from jax import lax
import numpy as np


D_MODEL = 1024
BATCH = 4
SEQ = 8192
DEPTH = 2

GLA_HEADS = 4
GLA_DK = 64
GLA_DV = 128
GLA_GATE_RANK = 16
GLA_TAU = 16.0
GLA_CHUNK = 64
MLA_HEADS = 8
MLA_Q_RANK = 256
MLA_KV_RANK = 128
MLA_NOPE = 64
MLA_ROPE = 32
MLA_V = 64
ROPE_THETA = 10000.0
DSA_HEADS = 8
DSA_HEAD_DIM = 64
IDX_HEADS = 8
IDX_DIM = 64
TOPK_MAX = 256
S5_GROUPS = 32
S5_GROUP_CH = 16
S5_STATE = 64
S5_DT_MIN = 0.001
S5_DT_MAX = 0.1
D_FF = 4 * D_MODEL
Q_BLOCK = 128
LN_EPS = 1e-5
DEEPNORM_ALPHA = (2 * DEPTH) ** 0.25
DEEPNORM_BETA = (8 * DEPTH) ** -0.25

L0_SPLITS = (GLA_HEADS * GLA_DK, GLA_HEADS * GLA_DK, GLA_HEADS * GLA_DV, GLA_GATE_RANK, GLA_HEADS * GLA_DV, MLA_Q_RANK, MLA_KV_RANK, MLA_ROPE)
L1_SPLITS = (DSA_HEADS * DSA_HEAD_DIM, DSA_HEADS * DSA_HEAD_DIM, DSA_HEADS * DSA_HEAD_DIM, IDX_HEADS * IDX_DIM, IDX_DIM, IDX_HEADS, S5_GROUPS * S5_GROUP_CH)
L0_IN = sum(L0_SPLITS)
L1_IN = sum(L1_SPLITS)
MIX0 = GLA_HEADS * GLA_DV + MLA_HEADS * MLA_V
MIX1 = DSA_HEADS * DSA_HEAD_DIM + S5_GROUPS * S5_GROUP_CH

kernel_name = 'hybrid_gla_mla_dsa_s5_block'


def _split(h, sizes):
    out = []
    off = 0
    for s in sizes:
        out.append(h[..., off:off + s])
        off += s
    return out


def _layernorm(x, g, b):
    xf = x.astype(jnp.float32)
    mu = jnp.mean(xf, -1, keepdims=True)
    var = jnp.mean(jnp.square(xf - mu), -1, keepdims=True)
    return ((xf - mu) * lax.rsqrt(var + LN_EPS) * g + b).astype(x.dtype)


def _rmsnorm(x, g):
    xf = x.astype(jnp.float32)
    return (xf * lax.rsqrt(jnp.mean(jnp.square(xf), -1, keepdims=True) + LN_EPS) * g).astype(x.dtype)


def _rope(t, pos):
    half = t.shape[-1] // 2
    inv_freq = ROPE_THETA ** (-jnp.arange(half, dtype=jnp.float32) / half)
    ang = pos.astype(jnp.float32)[:, :, None, None] * inv_freq
    cos, sin = jnp.cos(ang), jnp.sin(ang)
    tf = t.astype(jnp.float32)
    t1, t2 = tf[..., :half], tf[..., half:]
    return jnp.concatenate([t1 * cos - t2 * sin, t2 * cos + t1 * sin], -1).astype(t.dtype)


def _gla(q, k, v, log_a):
    b_, l_, h_, dk = q.shape
    dv = v.shape[-1]
    n_chunks = l_ // GLA_CHUNK

    def to_chunks(t):
        return t.astype(jnp.float32).reshape(b_, n_chunks, GLA_CHUNK, h_, t.shape[-1]).transpose(0, 3, 1, 2, 4)

    qc, kc, vc, gc = (to_chunks(t) for t in (q, k, v, log_a))
    cum = jnp.cumsum(gc, axis=3)
    cum_last = cum[:, :, :, -1:, :]
    q_dec = qc * jnp.exp(cum)
    k_inv = kc * jnp.exp(-cum)
    causal = jnp.tril(jnp.ones((GLA_CHUNK, GLA_CHUNK), dtype=bool))
    att = jnp.where(causal, jnp.einsum('bhnid,bhnjd->bhnij', q_dec, k_inv), 0.0)
    o_intra = jnp.einsum('bhnij,bhnje->bhnie', att, vc)
    k_end = kc * jnp.exp(cum_last - cum)
    chunk_kv = jnp.einsum('bhncd,bhnce->nbhde', k_end, vc)
    chunk_decay = jnp.exp(cum_last[:, :, :, 0, :]).transpose(2, 0, 1, 3)

    def step(state, inp):
        dec, kv = inp
        return dec[..., None] * state + kv, state

    s0 = jnp.zeros((b_, h_, dk, dv), jnp.float32)
    _, s_prev = lax.scan(step, s0, (chunk_decay, chunk_kv))
    o_inter = jnp.einsum('bhncd,nbhde->bhnce', q_dec, s_prev)
    return (o_intra + o_inter).transpose(0, 2, 3, 1, 4).reshape(b_, l_, h_, dv)


def _causal_block_attention(q, k, v, scale):
    b_, l_, h_, dq = q.shape
    dv = v.shape[-1]
    nb = l_ // Q_BLOCK
    qb = q.reshape(b_, nb, Q_BLOCK, h_, dq).transpose(1, 0, 2, 3, 4)
    key_pos = jnp.arange(l_)

    def one_block(args):
        qi, bi = args
        s = jnp.einsum('bqhd,bshd->bhqs', qi, k).astype(jnp.float32) * scale
        qpos = bi * Q_BLOCK + jnp.arange(Q_BLOCK)
        s = jnp.where((key_pos[None, :] <= qpos[:, None])[None, None], s, -jnp.inf)
        p = jax.nn.softmax(s, axis=-1)
        return jnp.einsum('bhqs,bshe->bqhe', p.astype(v.dtype), v)

    out = lax.map(one_block, (qb, jnp.arange(nb)))
    return out.transpose(1, 0, 2, 3, 4).reshape(b_, l_, h_, dv)


def _dsa(q, k, v, q_idx, k_idx, w_idx):
    b_, l_, h_, dh = q.shape
    topk = min(TOPK_MAX, l_ // 4)
    nb = l_ // Q_BLOCK

    def blocks(t):
        return t.reshape((b_, nb, Q_BLOCK) + t.shape[2:]).swapaxes(0, 1)

    key_pos = jnp.arange(l_)
    gather = jax.vmap(lambda kb, ib: kb[ib])

    def one_block(args):
        qi, qii, wi, bi = args
        qpos = bi * Q_BLOCK + jnp.arange(Q_BLOCK)
        logits = jnp.einsum('bqhd,bsd->bqhs', qii, k_idx).astype(jnp.float32) * (IDX_DIM ** -0.5)
        score = jnp.einsum('bqhs,bqh->bqs', jax.nn.relu(logits), wi.astype(jnp.float32))
        score = jnp.where((key_pos[None, :] <= qpos[:, None])[None], score, -jnp.inf)
        _, sel = lax.top_k(score, topk)
        sel_ok = sel <= qpos[None, :, None]
        k_sel = gather(k, sel)
        v_sel = gather(v, sel)
        s = jnp.einsum('bqhd,bqkhd->bhqk', qi, k_sel).astype(jnp.float32) * (dh ** -0.5)
        s = jnp.where(sel_ok[:, None], s, -jnp.inf)
        p = jax.nn.softmax(s, axis=-1)
        return jnp.einsum('bhqk,bqkhd->bqhd', p.astype(v.dtype), v_sel)

    out = lax.map(one_block, (blocks(q), blocks(q_idx), blocks(w_idx), jnp.arange(nb)))
    return out.swapaxes(0, 1).reshape(b_, l_, h_, dh)


def _s5(u, a_re, a_im, b_re, b_im, c_re, c_im, d_skip, log_step):
    b_, l_, _ = u.shape
    uf = u.astype(jnp.float32).reshape(b_, l_, S5_GROUPS, S5_GROUP_CH)
    lam_re = jnp.minimum(a_re.astype(jnp.float32), -1e-4)
    lam_im = a_im.astype(jnp.float32)
    dt = jnp.exp(log_step.astype(jnp.float32))[:, None]
    mag = jnp.exp(lam_re * dt)
    abar_re = mag * jnp.cos(lam_im * dt)
    abar_im = mag * jnp.sin(lam_im * dt)
    den = jnp.square(lam_re) + jnp.square(lam_im)
    nr = abar_re - 1.0
    ni = abar_im
    coef_re = (nr * lam_re + ni * lam_im) / den
    coef_im = (ni * lam_re - nr * lam_im) / den
    bf_re, bf_im = b_re.astype(jnp.float32), b_im.astype(jnp.float32)
    bbar_re = coef_re[..., None] * bf_re - coef_im[..., None] * bf_im
    bbar_im = coef_re[..., None] * bf_im + coef_im[..., None] * bf_re
    bu_re = jnp.einsum('blgc,gpc->blgp', uf, bbar_re)
    bu_im = jnp.einsum('blgc,gpc->blgp', uf, bbar_im)
    at_re = jnp.broadcast_to(abar_re, bu_re.shape)
    at_im = jnp.broadcast_to(abar_im, bu_im.shape)

    def combine(e1, e2):
        a1r, a1i, b1r, b1i = e1
        a2r, a2i, b2r, b2i = e2
        return (a2r * a1r - a2i * a1i,
                a2r * a1i + a2i * a1r,
                a2r * b1r - a2i * b1i + b2r,
                a2r * b1i + a2i * b1r + b2i)

    _, _, x_re, x_im = lax.associative_scan(combine, (at_re, at_im, bu_re, bu_im), axis=1)
    y = (jnp.einsum('blgp,gcp->blgc', x_re, c_re.astype(jnp.float32))
         - jnp.einsum('blgp,gcp->blgc', x_im, c_im.astype(jnp.float32)))
    y = y.reshape(b_, l_, S5_GROUPS * S5_GROUP_CH) + d_skip.astype(jnp.float32) * u.astype(jnp.float32)
    return y.astype(u.dtype)


def _mixer_gla_mla(x, positions, w_in, gla_wg2, gla_bg, gla_norm, mla_q_norm, mla_w_uq, mla_kv_norm, mla_w_ukv, w_out):
    b_, l_, _ = x.shape
    h = x @ w_in
    q, k, v, g_lr, r, c_q, c_kv, k_rope = _split(h, L0_SPLITS)
    gq = q.reshape(b_, l_, GLA_HEADS, GLA_DK) * (GLA_DK ** -0.5)
    gk = k.reshape(b_, l_, GLA_HEADS, GLA_DK)
    gv = v.reshape(b_, l_, GLA_HEADS, GLA_DV)
    log_a = (jax.nn.log_sigmoid((g_lr @ gla_wg2 + gla_bg).astype(jnp.float32)) / GLA_TAU).reshape(b_, l_, GLA_HEADS, GLA_DK)
    o = _rmsnorm(_gla(gq, gk, gv, log_a), gla_norm.reshape(GLA_HEADS, GLA_DV))
    o_gla = (o.reshape(b_, l_, GLA_HEADS * GLA_DV) * jax.nn.silu(r.astype(jnp.float32))).astype(x.dtype)
    qm = (_rmsnorm(c_q, mla_q_norm) @ mla_w_uq).reshape(b_, l_, MLA_HEADS, MLA_NOPE + MLA_ROPE)
    kvm = (_rmsnorm(c_kv, mla_kv_norm) @ mla_w_ukv).reshape(b_, l_, MLA_HEADS, MLA_NOPE + MLA_V)
    q_nope, q_rot = qm[..., :MLA_NOPE], _rope(qm[..., MLA_NOPE:], positions)
    k_nope, vm = kvm[..., :MLA_NOPE], kvm[..., MLA_NOPE:]
    k_rot = _rope(k_rope[:, :, None, :], positions)
    qf = jnp.concatenate([q_nope, q_rot], -1)
    kf = jnp.concatenate([k_nope, jnp.broadcast_to(k_rot, (b_, l_, MLA_HEADS, MLA_ROPE))], -1)
    o_mla = _causal_block_attention(qf, kf, vm, (MLA_NOPE + MLA_ROPE) ** -0.5).reshape(b_, l_, MLA_HEADS * MLA_V)
    return jnp.concatenate([o_gla, o_mla], -1) @ w_out


def _mixer_dsa_s5(x, w_in, s5_a_re, s5_a_im, s5_b_re, s5_b_im, s5_c_re, s5_c_im, s5_d, s5_log_step, glu_w, glu_b, w_out):
    b_, l_, _ = x.shape
    h = x @ w_in
    q, k, v, qi, ki, wi, u = _split(h, L1_SPLITS)
    o_dsa = _dsa(q.reshape(b_, l_, DSA_HEADS, DSA_HEAD_DIM),
                 k.reshape(b_, l_, DSA_HEADS, DSA_HEAD_DIM),
                 v.reshape(b_, l_, DSA_HEADS, DSA_HEAD_DIM),
                 qi.reshape(b_, l_, IDX_HEADS, IDX_DIM),
                 ki,
                 wi * (IDX_HEADS ** -0.5)).reshape(b_, l_, DSA_HEADS * DSA_HEAD_DIM)
    y = jax.nn.gelu(_s5(u, s5_a_re, s5_a_im, s5_b_re, s5_b_im, s5_c_re, s5_c_im, s5_d, s5_log_step))
    o_s5 = y * jax.nn.sigmoid(y @ glu_w + glu_b)
    return jnp.concatenate([o_dsa, o_s5], -1) @ w_out


def _sq_relu_mlp(x, w1, w2):
    return jnp.square(jax.nn.relu(x @ w1)) @ w2


def setup_inputs(seed: int = 0) -> dict:
    key = jax.random.key(seed)
    ks = list(jax.random.split(key, 64))

    def nrm(shape, scale):
        return jax.random.normal(ks.pop(), shape, jnp.float32) * scale

    def gain(n):
        return 1.0 + nrm((n,), 0.01)

    x = nrm((BATCH, SEQ, D_MODEL), 1.0)
    offs = jax.random.randint(ks.pop(), (BATCH, 1), 0, 1024, dtype=jnp.int32)
    positions = offs + jnp.arange(SEQ, dtype=jnp.int32)[None, :]
    s5w = S5_GROUPS * S5_GROUP_CH
    n_idx = jnp.arange(S5_STATE, dtype=jnp.float32)[None, :]
    return {
        'x': x,
        'positions': positions,
        'l0_w_in': nrm((D_MODEL, L0_IN), D_MODEL ** -0.5),
        'l0_gla_wg2': nrm((GLA_GATE_RANK, GLA_HEADS * GLA_DK), GLA_GATE_RANK ** -0.5),
        'l0_gla_bg': nrm((GLA_HEADS * GLA_DK,), 0.01),
        'l0_gla_norm': gain(GLA_HEADS * GLA_DV),
        'l0_mla_q_norm': gain(MLA_Q_RANK),
        'l0_mla_w_uq': nrm((MLA_Q_RANK, MLA_HEADS * (MLA_NOPE + MLA_ROPE)), MLA_Q_RANK ** -0.5),
        'l0_mla_kv_norm': gain(MLA_KV_RANK),
        'l0_mla_w_ukv': nrm((MLA_KV_RANK, MLA_HEADS * (MLA_NOPE + MLA_V)), MLA_KV_RANK ** -0.5),
        'l0_w_out': nrm((MIX0, D_MODEL), DEEPNORM_BETA * MIX0 ** -0.5),
        'l0_ln1_g': gain(D_MODEL),
        'l0_ln1_b': nrm((D_MODEL,), 0.01),
        'l0_mlp_w1': nrm((D_MODEL, D_FF), D_MODEL ** -0.5),
        'l0_mlp_w2': nrm((D_FF, D_MODEL), DEEPNORM_BETA * D_FF ** -0.5),
        'l0_ln2_g': gain(D_MODEL),
        'l0_ln2_b': nrm((D_MODEL,), 0.01),
        'l1_w_in': nrm((D_MODEL, L1_IN), D_MODEL ** -0.5),
        'l1_s5_a_re': -0.5 + nrm((S5_GROUPS, S5_STATE), 0.01),
        'l1_s5_a_im': math.pi * n_idx + nrm((S5_GROUPS, S5_STATE), 0.01),
        'l1_s5_b_re': nrm((S5_GROUPS, S5_STATE, S5_GROUP_CH), (2 * S5_GROUP_CH) ** -0.5),
        'l1_s5_b_im': nrm((S5_GROUPS, S5_STATE, S5_GROUP_CH), (2 * S5_GROUP_CH) ** -0.5),
        'l1_s5_c_re': nrm((S5_GROUPS, S5_GROUP_CH, S5_STATE), (2 * S5_STATE) ** -0.5),
        'l1_s5_c_im': nrm((S5_GROUPS, S5_GROUP_CH, S5_STATE), (2 * S5_STATE) ** -0.5),
        'l1_s5_d': nrm((s5w,), 1.0),
        'l1_s5_log_step': jax.random.uniform(ks.pop(), (S5_GROUPS,), jnp.float32, math.log(S5_DT_MIN), math.log(S5_DT_MAX)),
        'l1_glu_w': nrm((s5w, s5w), s5w ** -0.5),
        'l1_glu_b': nrm((s5w,), 0.01),
        'l1_w_out': nrm((MIX1, D_MODEL), DEEPNORM_BETA * MIX1 ** -0.5),
        'l1_ln1_g': gain(D_MODEL),
        'l1_ln1_b': nrm((D_MODEL,), 0.01),
        'l1_mlp_w1': nrm((D_MODEL, D_FF), D_MODEL ** -0.5),
        'l1_mlp_w2': nrm((D_FF, D_MODEL), DEEPNORM_BETA * D_FF ** -0.5),
        'l1_ln2_g': gain(D_MODEL),
        'l1_ln2_b': nrm((D_MODEL,), 0.01),
    }


def reference(x, positions,
              l0_w_in, l0_gla_wg2, l0_gla_bg, l0_gla_norm, l0_mla_q_norm, l0_mla_w_uq, l0_mla_kv_norm, l0_mla_w_ukv, l0_w_out,
              l0_ln1_g, l0_ln1_b, l0_mlp_w1, l0_mlp_w2, l0_ln2_g, l0_ln2_b,
              l1_w_in, l1_s5_a_re, l1_s5_a_im, l1_s5_b_re, l1_s5_b_im, l1_s5_c_re, l1_s5_c_im, l1_s5_d, l1_s5_log_step,
              l1_glu_w, l1_glu_b, l1_w_out,
              l1_ln1_g, l1_ln1_b, l1_mlp_w1, l1_mlp_w2, l1_ln2_g, l1_ln2_b):
    mixer_params = (
        (l0_w_in, l0_gla_wg2, l0_gla_bg, l0_gla_norm, l0_mla_q_norm, l0_mla_w_uq, l0_mla_kv_norm, l0_mla_w_ukv, l0_w_out),
        (l1_w_in, l1_s5_a_re, l1_s5_a_im, l1_s5_b_re, l1_s5_b_im, l1_s5_c_re, l1_s5_c_im, l1_s5_d, l1_s5_log_step, l1_glu_w, l1_glu_b, l1_w_out),
    )
    ffn_params = (
        (l0_ln1_g, l0_ln1_b, l0_mlp_w1, l0_mlp_w2, l0_ln2_g, l0_ln2_b),
        (l1_ln1_g, l1_ln1_b, l1_mlp_w1, l1_mlp_w2, l1_ln2_g, l1_ln2_b),
    )
    for i in range(DEPTH):
        ln1_g, ln1_b, w1, w2, ln2_g, ln2_b = ffn_params[i]
        if i % 2 == 0:
            mix = _mixer_gla_mla(x, positions, *mixer_params[i])
        else:
            mix = _mixer_dsa_s5(x, *mixer_params[i])
        x = _layernorm(DEEPNORM_ALPHA * x + mix, ln1_g, ln1_b)
        x = _layernorm(DEEPNORM_ALPHA * x + _sq_relu_mlp(x, w1, w2), ln2_g, ln2_b)
    return x
```

```python
import functools
import math

import jax
import jax.numpy as jnp
from jax import lax
from jax.experimental import pallas as pl
from jax.experimental.pallas import tpu as pltpu

BF16 = jnp.bfloat16
F32 = jnp.float32

D_MODEL = 1024
DEPTH = 2
GLA_HEADS, GLA_DK, GLA_DV, GLA_RANK, GLA_TAU, GLA_CHUNK = 4, 64, 128, 16, 16.0, 64
MLA_HEADS, MLA_Q_RANK, MLA_KV_RANK, MLA_NOPE, MLA_ROPE, MLA_V = 8, 256, 128, 64, 32, 64
ROPE_THETA = 10000.0
DSA_HEADS, DSA_DH, IDX_HEADS, IDX_DIM, TOPK_MAX = 8, 64, 8, 64, 256
S5_GROUPS, S5_CH, S5_STATE = 32, 16, 64
S5_W = S5_GROUPS * S5_CH
S5_N = S5_GROUPS * S5_STATE
D_FF = 4 * D_MODEL
LN_EPS = 1e-5
ALPHA = (2 * DEPTH) ** 0.25

LANES = 128
SUBLANES = 8
VMEM_LIMIT = 56 * 1024 * 1024
MASKED = -1e30
INT_MIN = -(2 ** 31)
NEG_INF_KEY = INT_MIN + 0x7FFFFF

L0_Q, L0_K, L0_V, L0_R, L0_CQ, L0_CKV, L0_MISC, L0_ROPE_A, L0_ROPE_B, L0_N = (
    0, 256, 512, 1024, 1536, 1792, 1920, 2048, 2176, 2304)
L1_Q, L1_K, L1_V, L1_QI, L1_KI, L1_NB = 0, 512, 1024, 1536, 2048, 2176
L1_U, L1_WI, L1_NF = 0, 512, 640


def _cparams(sem):
    return pltpu.CompilerParams(dimension_semantics=sem, vmem_limit_bytes=VMEM_LIMIT)


def _dot(a, b):
    return jnp.dot(a, b, preferred_element_type=F32)


def _dot_nt(a, b):
    return lax.dot_general(a, b, (((1,), (1,)), ((), ())), preferred_element_type=F32)


def _layernorm(v, g, b):
    mu = jnp.mean(v, -1, keepdims=True)
    d = v - mu
    var = jnp.mean(d * d, -1, keepdims=True)
    return d * lax.rsqrt(var + LN_EPS) * g + b


def _rmsnorm(v, g):
    return v * lax.rsqrt(jnp.mean(v * v, -1, keepdims=True) + LN_EPS) * g


def _proj_kernel(x_ref, w_ref, o_ref, *, chunk):
    xb = x_ref[...].astype(BF16)
    for j in range(0, o_ref.shape[1], chunk):
        o_ref[:, j:j + chunk] = _dot(xb, w_ref[:, j:j + chunk]).astype(o_ref.dtype)


def _proj(x2, w, out_dtype, tm=512):
    t, k = x2.shape
    n = w.shape[1]
    chunk = 256 if n % 256 == 0 else LANES
    return pl.pallas_call(
        functools.partial(_proj_kernel, chunk=chunk),
        grid=(t // tm,),
        in_specs=[pl.BlockSpec((tm, k), lambda i: (i, 0)),
                  pl.BlockSpec((k, n), lambda i: (0, 0))],
        out_specs=pl.BlockSpec((tm, n), lambda i: (i, 0)),
        out_shape=jax.ShapeDtypeStruct((t, n), out_dtype),
        compiler_params=_cparams(("parallel",)),
    )(x2, w)


def _gla_kernel(q_ref, k_ref, v_ref, r_ref, misc_ref, wg2_ref, bg_ref, gn_ref, o_ref, st_ref):
    c = GLA_CHUNK

    @pl.when(pl.program_id(1) == 0)
    def _():
        st_ref[...] = jnp.zeros_like(st_ref)

    z = _dot(misc_ref[:, :GLA_RANK].astype(BF16), wg2_ref[...]) + bg_ref[...]
    log_a = jax.nn.log_sigmoid(z) * (1.0 / GLA_TAU)
    row = lax.broadcasted_iota(jnp.int32, (c, c), 0)
    col = lax.broadcasted_iota(jnp.int32, (c, c), 1)
    causal = col <= row
    tril = jnp.where(causal, 1.0, 0.0).astype(BF16)
    a_hi = log_a.astype(BF16)
    rem = log_a - a_hi.astype(F32)
    a_mid = rem.astype(BF16)
    a_lo = (rem - a_mid.astype(F32)).astype(BF16)
    cum = _dot(tril, a_hi) + _dot(tril, a_mid) + _dot(tril, a_lo)
    cum_last = cum[c - 1:c, :]
    e_q = jnp.exp(cum)
    e_k = jnp.exp(-cum)
    e_end = jnp.exp(cum_last - cum)
    dec = jnp.exp(cum_last)
    q = q_ref[...] * (GLA_DK ** -0.5)
    k = k_ref[...]
    for h in range(GLA_HEADS):
        ks = slice(h * GLA_DK, (h + 1) * GLA_DK)
        vs = slice(h * GLA_DV, (h + 1) * GLA_DV)
        q_dec = (q[:, ks] * e_q[:, ks]).astype(BF16)
        k_inv = (k[:, ks] * e_k[:, ks]).astype(BF16)
        k_end = (k[:, ks] * e_end[:, ks]).astype(BF16)
        v = v_ref[:, vs]
        att = jnp.where(causal, _dot_nt(q_dec, k_inv), 0.0)
        st = st_ref[h]
        o = _dot(att.astype(BF16), v.astype(BF16)) + _dot_nt(q_dec, st.astype(BF16))
        st_ref[h] = st * dec[:, ks] + _dot(v.T.astype(BF16), k_end)
        o = _rmsnorm(o, gn_ref[:, vs])
        r = r_ref[:, vs]
        o_ref[:, vs] = (o * (r * jax.nn.sigmoid(r))).astype(o_ref.dtype)


def _gla(h0, wg2, bg, gn, b, l):
    nc = l // GLA_CHUNK
    c = GLA_CHUNK
    hk = GLA_HEADS * GLA_DK
    hv = GLA_HEADS * GLA_DV

    def rows(j):
        return lambda bi, ci: (bi * nc + ci, j)

    return pl.pallas_call(
        _gla_kernel,
        grid=(b, nc),
        in_specs=[pl.BlockSpec((c, hk), rows(L0_Q // hk)),
                  pl.BlockSpec((c, hk), rows(L0_K // hk)),
                  pl.BlockSpec((c, hv), rows(L0_V // hv)),
                  pl.BlockSpec((c, hv), rows(L0_R // hv)),
                  pl.BlockSpec((c, LANES), rows(L0_MISC // LANES)),
                  pl.BlockSpec((GLA_RANK, hk), lambda bi, ci: (0, 0)),
                  pl.BlockSpec((1, hk), lambda bi, ci: (0, 0)),
                  pl.BlockSpec((1, hv), lambda bi, ci: (0, 0))],
        out_specs=pl.BlockSpec((c, hv), rows(0)),
        out_shape=jax.ShapeDtypeStruct((b * l, hv), BF16),
        scratch_shapes=[pltpu.VMEM((GLA_HEADS, GLA_DV, GLA_DK), F32)],
        compiler_params=_cparams(("parallel", "arbitrary")),
    )(h0, h0, h0, h0, h0, wg2, bg, gn)


def _mla_proj_kernel(cq_ref, ckv_ref, ra_ref, rb_ref, pos_ref, invf_ref, qn_ref, kvn_ref,
                     wqa_ref, wqb_ref, wk_ref, wv_ref, q_ref, k_ref, v_ref):
    tm = cq_ref.shape[0]
    pos = pos_ref[0].astype(F32)
    ang = invf_ref[...] * pos
    ones = jnp.ones((MLA_NOPE, tm), F32)
    zeros = jnp.zeros((LANES - MLA_NOPE - MLA_ROPE, tm), F32)
    cos = jnp.concatenate([ones, jnp.cos(ang), zeros], axis=0).T
    sin = jnp.concatenate([zeros, zeros, jnp.sin(ang), zeros], axis=0).T
    cqn = _rmsnorm(cq_ref[...], qn_ref[...]).astype(BF16)
    ckvn = _rmsnorm(ckv_ref[...], kvn_ref[...]).astype(BF16)
    k_rot = ra_ref[...] * cos + rb_ref[...] * sin
    for h in range(MLA_HEADS):
        hs = slice(h * LANES, (h + 1) * LANES)
        qa = _dot(cqn, wqa_ref[:, hs])
        qb = _dot(cqn, wqb_ref[:, hs])
        q_ref[0, h] = (qa * cos + qb * sin).astype(q_ref.dtype)
        k_ref[0, h] = (_dot(ckvn, wk_ref[:, hs]) + k_rot).astype(k_ref.dtype)
        v_ref[0, h] = _dot(ckvn, wv_ref[:, h * MLA_V:(h + 1) * MLA_V]).astype(v_ref.dtype)


def _mla_proj(h0, pos3, invf, qn, kvn, wqa, wqb, wk, wv, b, l, tm=512):
    nt = l // tm

    def rows(j):
        return lambda bi, ti: (bi * nt + ti, j)

    def whole(a):
        return pl.BlockSpec(a.shape, lambda bi, ti: (0,) * a.ndim)

    hd = lambda bi, ti: (bi, 0, ti, 0)
    return pl.pallas_call(
        _mla_proj_kernel,
        grid=(b, nt),
        in_specs=[pl.BlockSpec((tm, MLA_Q_RANK), rows(L0_CQ // MLA_Q_RANK)),
                  pl.BlockSpec((tm, MLA_KV_RANK), rows(L0_CKV // MLA_KV_RANK)),
                  pl.BlockSpec((tm, LANES), rows(L0_ROPE_A // LANES)),
                  pl.BlockSpec((tm, LANES), rows(L0_ROPE_B // LANES)),
                  pl.BlockSpec((1, 1, tm), lambda bi, ti: (bi * nt + ti, 0, 0)),
                  whole(invf), whole(qn), whole(kvn), whole(wqa), whole(wqb), whole(wk), whole(wv)],
        out_specs=[pl.BlockSpec((1, MLA_HEADS, tm, LANES), hd),
                   pl.BlockSpec((1, MLA_HEADS, tm, LANES), hd),
                   pl.BlockSpec((1, MLA_HEADS, tm, MLA_V), hd)],
        out_shape=[jax.ShapeDtypeStruct((b, MLA_HEADS, l, LANES), BF16),
                   jax.ShapeDtypeStruct((b, MLA_HEADS, l, LANES), BF16),
                   jax.ShapeDtypeStruct((b, MLA_HEADS, l, MLA_V), BF16)],
        compiler_params=_cparams(("parallel", "parallel")),
    )(h0, h0, h0, h0, pos3, invf, qn, kvn, wqa, wqb, wk, wv)


def _mla_attn_kernel(q_ref, k_ref, v_ref, o_ref, m_ref, l_ref, acc_ref, *, scale):
    i = pl.program_id(1)
    j = pl.program_id(2)
    tq = q_ref.shape[2]
    tk = k_ref.shape[2]

    @pl.when(j == 0)
    def _():
        m_ref[...] = jnp.full_like(m_ref, MASKED)
        l_ref[...] = jnp.zeros_like(l_ref)
        acc_ref[...] = jnp.zeros_like(acc_ref)

    @pl.when(j <= i)
    def _():
        row = i * tq + lax.broadcasted_iota(jnp.int32, (tq, tk), 0)
        col = j * tk + lax.broadcasted_iota(jnp.int32, (tq, tk), 1)
        keep = col <= row
        for h in range(MLA_HEADS):
            s = _dot_nt(q_ref[0, h], k_ref[0, h]) * scale
            s = jnp.where(keep, s, MASKED)
            m_old = m_ref[h]
            m_new = jnp.maximum(m_old, jnp.max(s, axis=-1, keepdims=True))
            alpha = jnp.exp(m_old - m_new)
            p = jnp.exp(s - m_new)
            l_ref[h] = alpha * l_ref[h] + jnp.sum(p, axis=-1, keepdims=True)
            acc_ref[h] = alpha * acc_ref[h] + _dot(p.astype(BF16), v_ref[0, h])
            m_ref[h] = m_new

    @pl.when(j == i)
    def _():
        for h in range(MLA_HEADS):
            o_ref[0, :, h * MLA_V:(h + 1) * MLA_V] = (acc_ref[h] / l_ref[h]).astype(o_ref.dtype)


def _mla_attn(q, k, v, b, l, t=512):
    n = l // t
    kv_map = lambda bi, i, j: (bi, 0, jnp.minimum(i, j), 0)
    return pl.pallas_call(
        functools.partial(_mla_attn_kernel, scale=(MLA_NOPE + MLA_ROPE) ** -0.5),
        grid=(b, n, n),
        in_specs=[pl.BlockSpec((1, MLA_HEADS, t, LANES), lambda bi, i, j: (bi, 0, i, 0)),
                  pl.BlockSpec((1, MLA_HEADS, t, LANES), kv_map),
                  pl.BlockSpec((1, MLA_HEADS, t, MLA_V), kv_map)],
        out_specs=pl.BlockSpec((1, t, MLA_HEADS * MLA_V), lambda bi, i, j: (bi, i, 0)),
        out_shape=jax.ShapeDtypeStruct((b, l, MLA_HEADS * MLA_V), BF16),
        scratch_shapes=[pltpu.VMEM((MLA_HEADS, t, 1), F32),
                        pltpu.VMEM((MLA_HEADS, t, 1), F32),
                        pltpu.VMEM((MLA_HEADS, t, MLA_V), F32)],
        compiler_params=_cparams(("parallel", "parallel", "arbitrary")),
    )(q, k, v)


def _out_ln_kernel(a_ref, b_ref, x_ref, w_ref, g_ref, bias_ref, o_ref):
    na = a_ref.shape[1]
    mix = _dot(a_ref[...], w_ref[:na, :]) + _dot(b_ref[...], w_ref[na:, :])
    o_ref[...] = _layernorm(ALPHA * x_ref[...] + mix, g_ref[...], bias_ref[...])


def _out_ln(a, bb, x2, w, g, bias, tm=512):
    t = x2.shape[0]
    na, nb = a.shape[1], bb.shape[1]
    return pl.pallas_call(
        _out_ln_kernel,
        grid=(t // tm,),
        in_specs=[pl.BlockSpec((tm, na), lambda i: (i, 0)),
                  pl.BlockSpec((tm, nb), lambda i: (i, 0)),
                  pl.BlockSpec((tm, D_MODEL), lambda i: (i, 0)),
                  pl.BlockSpec((na + nb, D_MODEL), lambda i: (0, 0)),
                  pl.BlockSpec((1, D_MODEL), lambda i: (0, 0)),
                  pl.BlockSpec((1, D_MODEL), lambda i: (0, 0))],
        out_specs=pl.BlockSpec((tm, D_MODEL), lambda i: (i, 0)),
        out_shape=jax.ShapeDtypeStruct((t, D_MODEL), F32),
        compiler_params=_cparams(("parallel",)),
    )(a, bb, x2, w, g, bias)


def _mlp_ln_kernel(x_ref, w1_ref, w2_ref, g_ref, bias_ref, o_ref, acc_ref):
    j = pl.program_id(1)

    @pl.when(j == 0)
    def _():
        acc_ref[...] = jnp.zeros_like(acc_ref)

    hid = jnp.maximum(_dot(x_ref[...].astype(BF16), w1_ref[...]), 0.0)
    acc_ref[...] += _dot((hid * hid).astype(BF16), w2_ref[...])

    @pl.when(j == pl.num_programs(1) - 1)
    def _():
        o_ref[...] = _layernorm(ALPHA * x_ref[...] + acc_ref[...], g_ref[...], bias_ref[...])


def _mlp_ln(x2, w1, w2, g, bias, tm=512, tf=1024):
    t = x2.shape[0]
    return pl.pallas_call(
        _mlp_ln_kernel,
        grid=(t // tm, D_FF // tf),
        in_specs=[pl.BlockSpec((tm, D_MODEL), lambda i, j: (i, 0)),
                  pl.BlockSpec((D_MODEL, tf), lambda i, j: (0, j)),
                  pl.BlockSpec((tf, D_MODEL), lambda i, j: (j, 0)),
                  pl.BlockSpec((1, D_MODEL), lambda i, j: (0, 0)),
                  pl.BlockSpec((1, D_MODEL), lambda i, j: (0, 0))],
        out_specs=pl.BlockSpec((tm, D_MODEL), lambda i, j: (i, 0)),
        out_shape=jax.ShapeDtypeStruct((t, D_MODEL), F32),
        scratch_shapes=[pltpu.VMEM((tm, D_MODEL), F32)],
        compiler_params=_cparams(("parallel", "arbitrary")),
    )(x2, w1, w2, g, bias)


def _dsa_kernel(q_ref, qi_ref, wi_ref, k_ref, v_ref, ki_ref, o_ref, key_ref, bias_ref, *, topk, tk, seq):
    tq = q_ref.shape[0]
    q0 = pl.program_id(1) * tq
    nkb = (q0 + tq + tk - 1) // tk
    nsub = tk // LANES
    row = q0 + lax.broadcasted_iota(jnp.int32, (tq, tk), 0)
    col0 = lax.broadcasted_iota(jnp.int32, (tq, tk), 1)
    rowc = q0 + lax.broadcasted_iota(jnp.int32, (tq, 1), 0)

    def kslice(kb):
        return pl.ds(pl.multiple_of(kb * tk, tk), tk)

    w_idx = wi_ref[...] * (IDX_HEADS ** -0.5)
    qi = qi_ref[...]

    def score_block(kb, carry):
        kib = ki_ref[0, kslice(kb), :IDX_DIM]
        sc = jnp.zeros((tq, tk), F32)
        for h in range(IDX_HEADS):
            lg = _dot_nt(qi[:, h * IDX_DIM:(h + 1) * IDX_DIM], kib) * (IDX_DIM ** -0.5)
            sc = sc + jnp.maximum(lg, 0.0) * w_idx[:, h:h + 1]
        sc = jnp.where(kb * tk + col0 <= row, sc, -jnp.inf)
        bits = pltpu.bitcast(sc, jnp.int32)
        key_ref[:, kslice(kb)] = jnp.where(bits < 0, bits ^ jnp.int32(0x7FFFFFFF), bits)
        return carry

    lax.fori_loop(0, nkb, score_block, 0)

    def count(pred):
        def body(kb, acc):
            keys = key_ref[:, kslice(kb)]
            for c in range(nsub):
                hit = pred(keys[:, c * LANES:(c + 1) * LANES], kb * tk + c * LANES)
                acc = acc + jnp.where(hit, 1, 0)
            return acc
        acc = lax.fori_loop(0, nkb, body, jnp.zeros((tq, LANES), jnp.int32))
        return jnp.sum(acc, axis=1, keepdims=True)

    n_nonneg = count(lambda ks, c0: ks >= 0)
    prefix = jnp.where(n_nonneg >= topk, 0, INT_MIN).astype(jnp.int32)

    def bit_step(b, prefix):
        cand = prefix | jnp.left_shift(jnp.int32(1), 30 - b)
        n_ge = count(lambda ks, c0: ks >= cand)
        return jnp.where(n_ge >= topk, cand, prefix)

    thr = lax.fori_loop(0, 31, bit_step, prefix)
    n_gt = count(lambda ks, c0: ks > thr)
    n_eq = count(lambda ks, c0: ks == thr)
    need = topk - n_gt
    finite = thr != NEG_INF_KEY
    lane = lax.broadcasted_iota(jnp.int32, (tq, LANES), 1)

    def tie_cut():
        def step(b, ans):
            cand = ans | jnp.left_shift(jnp.int32(1), nbits - 1 - b)
            n_before = count(lambda ks, c0: (ks == thr) & (c0 + lane < cand))
            return jnp.where(n_before < need, cand, ans)
        nbits = max(1, (seq - 1).bit_length())
        return lax.fori_loop(0, nbits, step, jnp.zeros((tq, 1), jnp.int32))

    has_tie = jnp.max(jnp.where(finite & (n_eq > need), 1, 0)) > 0
    cut = lax.cond(has_tie, tie_cut, lambda: jnp.full((tq, 1), seq, jnp.int32))
    cut = jnp.where(finite, cut, rowc)

    def bias_block(kb, carry):
        keys = key_ref[:, kslice(kb)]
        sel = (keys > thr) | ((keys == thr) & (kb * tk + col0 <= cut))
        bias_ref[:, kslice(kb)] = jnp.where(sel, 0.0, MASKED)
        return carry

    lax.fori_loop(0, nkb, bias_block, 0)

    for h in range(DSA_HEADS):
        hs = slice(h * DSA_DH, (h + 1) * DSA_DH)
        qh = q_ref[:, hs]

        def attend(kb, carry, hs=hs, qh=qh):
            m_old, l_old, acc = carry
            s = _dot_nt(qh, k_ref[0, kslice(kb), hs]) * (DSA_DH ** -0.5) + bias_ref[:, kslice(kb)]
            m_new = jnp.maximum(m_old, jnp.max(s, axis=-1, keepdims=True))
            alpha = jnp.exp(m_old - m_new)
            p = jnp.exp(s - m_new)
            l_new = alpha * l_old + jnp.sum(p, axis=-1, keepdims=True)
            acc = alpha * acc + _dot(p.astype(BF16), v_ref[0, kslice(kb), hs])
            return m_new, l_new, acc

        init = (jnp.full((tq, 1), MASKED, F32), jnp.zeros((tq, 1), F32), jnp.zeros((tq, DSA_DH), F32))
        _, l_fin, acc = lax.fori_loop(0, nkb, attend, init)
        o_ref[:, hs] = (acc / l_fin).astype(o_ref.dtype)


def _dsa(hb, hf, b, l, tq=128, tk=512):
    tk = min(tk, l)
    nq = l // tq
    topk = min(TOPK_MAX, l // 4)
    hd = DSA_HEADS * DSA_DH
    hi = IDX_HEADS * IDX_DIM

    def rows(j):
        return lambda bi, qi: (bi * nq + qi, j)

    def seqblock(j):
        return lambda bi, qi: (bi, 0, j)

    hb3 = hb.reshape(b, l, L1_NB)
    return pl.pallas_call(
        functools.partial(_dsa_kernel, topk=topk, tk=tk, seq=l),
        grid=(b, nq),
        in_specs=[pl.BlockSpec((tq, hd), rows(L1_Q // hd)),
                  pl.BlockSpec((tq, hi), rows(L1_QI // hi)),
                  pl.BlockSpec((tq, LANES), rows(L1_WI // LANES)),
                  pl.BlockSpec((1, l, hd), seqblock(L1_K // hd), pipeline_mode=pl.Buffered(1)),
                  pl.BlockSpec((1, l, hd), seqblock(L1_V // hd), pipeline_mode=pl.Buffered(1)),
                  pl.BlockSpec((1, l, LANES), seqblock(L1_KI // LANES))],
        out_specs=pl.BlockSpec((tq, hd), rows(0)),
        out_shape=jax.ShapeDtypeStruct((b * l, hd), BF16),
        scratch_shapes=[pltpu.VMEM((tq, l), jnp.int32), pltpu.VMEM((tq, l), F32)],
        compiler_params=_cparams(("parallel", "arbitrary")),
    )(hb, hb, hf, hb3, hb3, hb3)


def _s5_kernel(u_ref, wbr_ref, wbi_ref, pwr_ref, pwi_ref, wcr_ref, wci_ref, d_ref, gw_ref, gb_ref,
               o_ref, xr_ref, xi_ref, cr_ref, ci_ref):
    ts = u_ref.shape[0]

    @pl.when(pl.program_id(1) == 0)
    def _():
        cr_ref[...] = jnp.zeros_like(cr_ref)
        ci_ref[...] = jnp.zeros_like(ci_ref)

    u = u_ref[...]
    ub = u.astype(BF16)
    xr = _dot(ub, wbr_ref[...])
    xi = _dot(ub, wbi_ref[...])
    sub = lax.broadcasted_iota(jnp.int32, (ts, S5_N), 0) % SUBLANES
    for d in (1, 2, 4):
        ar = pwr_ref[d - 1:d, :]
        ai = pwi_ref[d - 1:d, :]
        sr = jnp.where(sub >= d, pltpu.roll(xr, d, axis=0), 0.0)
        si = jnp.where(sub >= d, pltpu.roll(xi, d, axis=0), 0.0)
        xr, xi = xr + (ar * sr - ai * si), xi + (ar * si + ai * sr)
    xr_ref[...] = xr
    xi_ref[...] = xi
    pwr = pwr_ref[...]
    pwi = pwi_ref[...]

    def group(g, carry):
        cr, ci = carry
        rs = pl.ds(pl.multiple_of(g * SUBLANES, SUBLANES), SUBLANES)
        gr = xr_ref[rs, :] + (pwr * cr - pwi * ci)
        gi = xi_ref[rs, :] + (pwr * ci + pwi * cr)
        xr_ref[rs, :] = gr
        xi_ref[rs, :] = gi
        return gr[SUBLANES - 1:SUBLANES, :], gi[SUBLANES - 1:SUBLANES, :]

    cr, ci = lax.fori_loop(0, ts // SUBLANES, group, (cr_ref[...], ci_ref[...]))
    cr_ref[...] = cr
    ci_ref[...] = ci
    y = _dot(xr_ref[...].astype(BF16), wcr_ref[...]) - _dot(xi_ref[...].astype(BF16), wci_ref[...])
    y = jax.nn.gelu(y + d_ref[...] * u)
    gate = jax.nn.sigmoid(_dot(y.astype(BF16), gw_ref[...]) + gb_ref[...])
    o_ref[...] = (y * gate).astype(o_ref.dtype)


def _s5(hf, wbr, wbi, pwr, pwi, wcr, wci, d, gw, gb, b, l, ts=256):
    nt = l // ts

    def whole(a):
        return pl.BlockSpec(a.shape, lambda bi, ti: (0,) * a.ndim)

    return pl.pallas_call(
        _s5_kernel,
        grid=(b, nt),
        in_specs=[pl.BlockSpec((ts, S5_W), lambda bi, ti: (bi * nt + ti, L1_U // S5_W)),
                  whole(wbr), whole(wbi), whole(pwr), whole(pwi), whole(wcr), whole(wci),
                  whole(d), whole(gw), whole(gb)],
        out_specs=pl.BlockSpec((ts, S5_W), lambda bi, ti: (bi * nt + ti, 0)),
        out_shape=jax.ShapeDtypeStruct((b * l, S5_W), BF16),
        scratch_shapes=[pltpu.VMEM((ts, S5_N), F32), pltpu.VMEM((ts, S5_N), F32),
                        pltpu.VMEM((1, S5_N), F32), pltpu.VMEM((1, S5_N), F32)],
        compiler_params=_cparams(("parallel", "arbitrary")),
    )(hf, wbr, wbi, pwr, pwi, wcr, wci, d, gw, gb)


def _pad_cols(w, n):
    return jnp.pad(w, ((0, 0), (0, n - w.shape[1])))


def _l0_weights(w_in, w_uq, w_ukv):
    q, k, v, g_lr, r, c_q, c_kv, k_rope = jnp.split(
        w_in, [256, 512, 1024, 1040, 1552, 1808, 1936], axis=1)
    half = MLA_ROPE // 2
    t1, t2 = k_rope[:, :half], k_rope[:, half:]
    z = lambda n: jnp.zeros((w_in.shape[0], n), w_in.dtype)
    tail = LANES - MLA_NOPE - MLA_ROPE
    rope_a = jnp.concatenate([z(MLA_NOPE), t1, t2, z(tail)], axis=1)
    rope_b = jnp.concatenate([z(MLA_NOPE), -t2, t1, z(tail)], axis=1)
    w0 = jnp.concatenate([q, k, v, r, c_q, c_kv, _pad_cols(g_lr, LANES), rope_a, rope_b], axis=1)
    uq = w_uq.reshape(MLA_Q_RANK, MLA_HEADS, MLA_NOPE + MLA_ROPE)
    nope, r1, r2 = uq[..., :MLA_NOPE], uq[..., MLA_NOPE:MLA_NOPE + half], uq[..., MLA_NOPE + half:]
    zq = lambda n: jnp.zeros((MLA_Q_RANK, MLA_HEADS, n), w_uq.dtype)
    wqa = jnp.concatenate([nope, r1, r2, zq(tail)], axis=-1).reshape(MLA_Q_RANK, MLA_HEADS * LANES)
    wqb = jnp.concatenate([zq(MLA_NOPE), -r2, r1, zq(tail)], axis=-1).reshape(MLA_Q_RANK, MLA_HEADS * LANES)
    ukv = w_ukv.reshape(MLA_KV_RANK, MLA_HEADS, MLA_NOPE + MLA_V)
    wk = jnp.pad(ukv[..., :MLA_NOPE], ((0, 0), (0, 0), (0, LANES - MLA_NOPE))).reshape(MLA_KV_RANK, MLA_HEADS * LANES)
    wv = ukv[..., MLA_NOPE:].reshape(MLA_KV_RANK, MLA_HEADS * MLA_V)
    return w0.astype(BF16), wqa.astype(BF16), wqb.astype(BF16), wk.astype(BF16), wv.astype(BF16)


def _l1_weights(w_in):
    q, k, v, qi, ki, wi, u = jnp.split(w_in, [512, 1024, 1536, 2048, 2112, 2120], axis=1)
    wb = jnp.concatenate([q, k, v, qi, _pad_cols(ki, LANES)], axis=1)
    wf = jnp.concatenate([u, _pad_cols(wi, LANES)], axis=1)
    return wb.astype(BF16), wf.astype(BF16)


def _s5_weights(a_re, a_im, b_re, b_im, c_re, c_im, log_step):
    lam_re = jnp.minimum(a_re, -1e-4)
    lam_im = a_im
    dt = jnp.exp(log_step)[:, None]
    mag = jnp.exp(lam_re * dt)
    abar_re = mag * jnp.cos(lam_im * dt)
    abar_im = mag * jnp.sin(lam_im * dt)
    den = jnp.square(lam_re) + jnp.square(lam_im)
    nr = abar_re - 1.0
    ni = abar_im
    coef_re = (nr * lam_re + ni * lam_im) / den
    coef_im = (ni * lam_re - nr * lam_im) / den
    bbar_re = coef_re[..., None] * b_re - coef_im[..., None] * b_im
    bbar_im = coef_re[..., None] * b_im + coef_im[..., None] * b_re
    eye = jnp.eye(S5_GROUPS, dtype=F32)
    blockdiag_in = lambda t: jnp.einsum('gpc,gh->gchp', t, eye).reshape(S5_W, S5_N).astype(BF16)
    blockdiag_out = lambda t: jnp.einsum('gcp,gh->gphc', t, eye).reshape(S5_N, S5_W).astype(BF16)
    pr, pi = [abar_re.reshape(1, S5_N)], [abar_im.reshape(1, S5_N)]
    for _ in range(SUBLANES - 1):
        pr, pi = (pr + [pr[-1] * pr[0] - pi[-1] * pi[0]], pi + [pr[-1] * pi[0] + pi[-1] * pr[0]])
    return (blockdiag_in(bbar_re), blockdiag_in(bbar_im), jnp.concatenate(pr, 0), jnp.concatenate(pi, 0),
            blockdiag_out(c_re), blockdiag_out(c_im))


def kernel(x, positions, l0_w_in, l0_gla_wg2, l0_gla_bg, l0_gla_norm, l0_mla_q_norm, l0_mla_w_uq, l0_mla_kv_norm, l0_mla_w_ukv, l0_w_out, l0_ln1_g, l0_ln1_b, l0_mlp_w1, l0_mlp_w2, l0_ln2_g, l0_ln2_b, l1_w_in, l1_s5_a_re, l1_s5_a_im, l1_s5_b_re, l1_s5_b_im, l1_s5_c_re, l1_s5_c_im, l1_s5_d, l1_s5_log_step, l1_glu_w, l1_glu_b, l1_w_out, l1_ln1_g, l1_ln1_b, l1_mlp_w1, l1_mlp_w2, l1_ln2_g, l1_ln2_b):
    b, l, _ = x.shape
    t = b * l
    row = lambda a: a.reshape(1, -1)
    x2 = x.reshape(t, D_MODEL)

    w0, wqa, wqb, wk, wv = _l0_weights(l0_w_in, l0_mla_w_uq, l0_mla_w_ukv)
    h0 = _proj(x2, w0, F32)
    o_gla = _gla(h0, l0_gla_wg2.astype(BF16), row(l0_gla_bg), row(l0_gla_norm), b, l)
    tm = min(512, l)
    half = MLA_ROPE // 2
    inv_freq = ROPE_THETA ** (-jnp.arange(half, dtype=F32) / half)
    invf = jnp.concatenate([inv_freq, inv_freq]).reshape(MLA_ROPE, 1)
    pos3 = positions.reshape(t // tm, 1, tm)
    qm, km, vm = _mla_proj(h0, pos3, invf, row(l0_mla_q_norm), row(l0_mla_kv_norm), wqa, wqb, wk, wv, b, l, tm)
    o_mla = _mla_attn(qm, km, vm, b, l, tm).reshape(t, MLA_HEADS * MLA_V)
    x2 = _out_ln(o_gla, o_mla, x2, l0_w_out.astype(BF16), row(l0_ln1_g), row(l0_ln1_b))
    x2 = _mlp_ln(x2, l0_mlp_w1.astype(BF16), l0_mlp_w2.astype(BF16), row(l0_ln2_g), row(l0_ln2_b))

    w1b, w1f = _l1_weights(l1_w_in)
    hb = _proj(x2, w1b, BF16)
    hf = _proj(x2, w1f, F32)
    o_dsa = _dsa(hb, hf, b, l)
    s5w = _s5_weights(l1_s5_a_re, l1_s5_a_im, l1_s5_b_re, l1_s5_b_im, l1_s5_c_re, l1_s5_c_im, l1_s5_log_step)
    o_s5 = _s5(hf, *s5w, row(l1_s5_d), l1_glu_w.astype(BF16), row(l1_glu_b), b, l, min(256, l))
    x2 = _out_ln(o_dsa, o_s5, x2, l1_w_out.astype(BF16), row(l1_ln1_g), row(l1_ln1_b))
    x2 = _mlp_ln(x2, l1_mlp_w1.astype(BF16), l1_mlp_w2.astype(BF16), row(l1_ln2_g), row(l1_ln2_b))
    return x2.reshape(b, l, D_MODEL)
```

```python
import functools
import math

import jax
import jax.numpy as jnp
from jax import lax
from jax.experimental import pallas as pl
from jax.experimental.pallas import tpu as pltpu

BF16 = jnp.bfloat16
F32 = jnp.float32

D_MODEL = 1024
DEPTH = 2
GLA_HEADS, GLA_DK, GLA_DV, GLA_RANK, GLA_TAU, GLA_CHUNK = 4, 64, 128, 16, 16.0, 64
MLA_HEADS, MLA_Q_RANK, MLA_KV_RANK, MLA_NOPE, MLA_ROPE, MLA_V = 8, 256, 128, 64, 32, 64
ROPE_THETA = 10000.0
DSA_HEADS, DSA_DH, IDX_HEADS, IDX_DIM, TOPK_MAX = 8, 64, 8, 64, 256
S5_GROUPS, S5_CH, S5_STATE = 32, 16, 64
S5_W = S5_GROUPS * S5_CH
S5_N = S5_GROUPS * S5_STATE
D_FF = 4 * D_MODEL
LN_EPS = 1e-5
ALPHA = (2 * DEPTH) ** 0.25

LANES = 128
SUBLANES = 8
VMEM_LIMIT = 56 * 1024 * 1024
MASKED = -1e30
INT_MIN = -(2 ** 31)
NEG_INF_KEY = INT_MIN + 0x7FFFFF
LOG2E = math.log2(math.e)
SCORES_AHEAD = 3
DV_ONES = 16
MLA_QSCALE = (MLA_NOPE + MLA_ROPE) ** -0.5 * LOG2E
DSA_QSCALE = DSA_DH ** -0.5 * LOG2E

L0_Q, L0_K, L0_V, L0_R, L0_CQ, L0_CKV, L0_MISC, L0_ROPE_A, L0_ROPE_B, L0_N = (
    0, 256, 512, 1024, 1536, 1792, 1920, 2048, 2176, 2304)
L1_Q, L1_QI, L1_K, L1_KI, L1_V, L1_NB = 0, 512, 1024, 1536, 1664, 2176
L1_U, L1_WI, L1_NF = 0, 512, 640


def _cparams(sem):
    return pltpu.CompilerParams(dimension_semantics=sem, vmem_limit_bytes=VMEM_LIMIT)


def _dot(a, b):
    return jnp.dot(a, b, preferred_element_type=F32)


def _dot_nt(a, b):
    return lax.dot_general(a, b, (((1,), (1,)), ((), ())), preferred_element_type=F32)


def _layernorm(v, g, b):
    mu = jnp.mean(v, -1, keepdims=True)
    d = v - mu
    var = jnp.mean(d * d, -1, keepdims=True)
    return d * lax.rsqrt(var + LN_EPS) * g + b


def _rmsnorm(v, g):
    return v * lax.rsqrt(jnp.mean(v * v, -1, keepdims=True) + LN_EPS) * g


def _whole(a, nargs):
    return pl.BlockSpec(a.shape, lambda *_: (0,) * a.ndim)


def _proj_kernel(x_ref, w_ref, o_ref, *, chunk):
    xb = x_ref[...].astype(BF16)
    for j in range(0, o_ref.shape[1], chunk):
        o_ref[:, j:j + chunk] = _dot(xb, w_ref[:, j:j + chunk]).astype(o_ref.dtype)


def _proj(x2, w, out_dtype, tm=512):
    t, k = x2.shape
    n = w.shape[1]
    return pl.pallas_call(
        functools.partial(_proj_kernel, chunk=256),
        grid=(t // tm,),
        in_specs=[pl.BlockSpec((tm, k), lambda i: (i, 0)),
                  pl.BlockSpec((k, n), lambda i: (0, 0))],
        out_specs=pl.BlockSpec((tm, n), lambda i: (i, 0)),
        out_shape=jax.ShapeDtypeStruct((t, n), out_dtype),
        compiler_params=_cparams(("parallel",)),
    )(x2, w)


def _gla_kernel(q_ref, k_ref, v_ref, r_ref, misc_ref, wg2_ref, bg_ref, gn_ref, o_ref, st_ref):
    c = GLA_CHUNK

    @pl.when(pl.program_id(1) == 0)
    def _():
        st_ref[...] = jnp.zeros_like(st_ref)

    z = _dot(misc_ref[:, :GLA_RANK].astype(BF16), wg2_ref[...]) + bg_ref[...]
    log_a = jax.nn.log_sigmoid(z) * (1.0 / GLA_TAU)
    row = lax.broadcasted_iota(jnp.int32, (c, c), 0)
    col = lax.broadcasted_iota(jnp.int32, (c, c), 1)
    causal = col <= row
    tril = jnp.where(causal, 1.0, 0.0).astype(BF16)
    a_hi = log_a.astype(BF16)
    rem = log_a - a_hi.astype(F32)
    a_mid = rem.astype(BF16)
    a_lo = (rem - a_mid.astype(F32)).astype(BF16)
    cum = _dot(tril, a_hi) + _dot(tril, a_mid) + _dot(tril, a_lo)
    cum_last = cum[c - 1:c, :]
    e_q = jnp.exp(cum)
    e_k = jnp.exp(-cum)
    e_end = jnp.exp(cum_last - cum)
    dec = jnp.exp(cum_last)
    q = q_ref[...] * (GLA_DK ** -0.5)
    k = k_ref[...]
    for h in range(GLA_HEADS):
        ks = slice(h * GLA_DK, (h + 1) * GLA_DK)
        vs = slice(h * GLA_DV, (h + 1) * GLA_DV)
        q_dec = (q[:, ks] * e_q[:, ks]).astype(BF16)
        k_inv = (k[:, ks] * e_k[:, ks]).astype(BF16)
        k_end = (k[:, ks] * e_end[:, ks]).astype(BF16)
        v = v_ref[:, vs]
        att = jnp.where(causal, _dot_nt(q_dec, k_inv), 0.0)
        st = st_ref[h]
        o = _dot(att.astype(BF16), v.astype(BF16)) + _dot_nt(q_dec, st.astype(BF16))
        st_ref[h] = st * dec[:, ks] + _dot(v.T.astype(BF16), k_end)
        o = _rmsnorm(o, gn_ref[:, vs])
        r = r_ref[:, vs]
        o_ref[:, vs] = (o * (r * jax.nn.sigmoid(r))).astype(o_ref.dtype)


def _gla(h0, wg2, bg, gn, b, l):
    nc = l // GLA_CHUNK
    c = GLA_CHUNK
    hk = GLA_HEADS * GLA_DK
    hv = GLA_HEADS * GLA_DV

    def rows(j):
        return lambda bi, ci: (bi * nc + ci, j)

    return pl.pallas_call(
        _gla_kernel,
        grid=(b, nc),
        in_specs=[pl.BlockSpec((c, hk), rows(L0_Q // hk)),
                  pl.BlockSpec((c, hk), rows(L0_K // hk)),
                  pl.BlockSpec((c, hv), rows(L0_V // hv)),
                  pl.BlockSpec((c, hv), rows(L0_R // hv)),
                  pl.BlockSpec((c, LANES), rows(L0_MISC // LANES)),
                  pl.BlockSpec((GLA_RANK, hk), lambda bi, ci: (0, 0)),
                  pl.BlockSpec((1, hk), lambda bi, ci: (0, 0)),
                  pl.BlockSpec((1, hv), lambda bi, ci: (0, 0))],
        out_specs=pl.BlockSpec((c, hv), rows(0)),
        out_shape=jax.ShapeDtypeStruct((b * l, hv), BF16),
        scratch_shapes=[pltpu.VMEM((GLA_HEADS, GLA_DV, GLA_DK), F32)],
        compiler_params=_cparams(("parallel", "arbitrary")),
    )(h0, h0, h0, h0, h0, wg2, bg, gn)


def _mla_proj_kernel(cq_ref, ckv_ref, ra_ref, rb_ref, pos_ref, invf_ref, qn_ref, kvn_ref,
                     wqa_ref, wqb_ref, wk_ref, wv_ref, qt_ref, k_ref, vt_ref):
    tm = cq_ref.shape[0]
    pos = pos_ref[0].astype(F32)
    ang = invf_ref[...] * pos
    ones = jnp.ones((MLA_NOPE, tm), F32)
    pad = jnp.zeros((LANES - MLA_NOPE - MLA_ROPE, tm), F32)
    cos = jnp.concatenate([ones, jnp.cos(ang), pad], axis=0).T
    sin = jnp.concatenate([jnp.zeros((MLA_NOPE, tm), F32), jnp.sin(ang), pad], axis=0).T
    cqn = _rmsnorm(cq_ref[...], qn_ref[...]).astype(BF16)
    ckvn = _rmsnorm(ckv_ref[...], kvn_ref[...]).astype(BF16)
    k_rot = ra_ref[...] * cos + rb_ref[...] * sin
    for h in range(MLA_HEADS):
        hs = slice(h * LANES, (h + 1) * LANES)
        qh = _dot(cqn, wqa_ref[:, hs]) * cos + _dot(cqn, wqb_ref[:, hs]) * sin
        qt_ref[0, h] = (qh * MLA_QSCALE).T.astype(qt_ref.dtype)
        k_ref[0, h] = (_dot(ckvn, wk_ref[:, hs]) + k_rot).astype(k_ref.dtype)
    vt_ref[0] = _values_with_ones(_dot(ckvn, wv_ref[...]), MLA_HEADS, MLA_V).astype(vt_ref.dtype)


def _mla_proj(h0, pos3, invf, qn, kvn, wqa, wqb, wk, wv, b, l, tm):
    nt = l // tm

    def rows(j):
        return lambda bi, ti: (bi * nt + ti, j)

    consts = (invf, qn, kvn, wqa, wqb, wk, wv)
    return pl.pallas_call(
        _mla_proj_kernel,
        grid=(b, nt),
        in_specs=[pl.BlockSpec((tm, MLA_Q_RANK), rows(L0_CQ // MLA_Q_RANK)),
                  pl.BlockSpec((tm, MLA_KV_RANK), rows(L0_CKV // MLA_KV_RANK)),
                  pl.BlockSpec((tm, LANES), rows(L0_ROPE_A // LANES)),
                  pl.BlockSpec((tm, LANES), rows(L0_ROPE_B // LANES)),
                  pl.BlockSpec((1, 1, tm), lambda bi, ti: (bi * nt + ti, 0, 0))]
                 + [_whole(a, 2) for a in consts],
        out_specs=[pl.BlockSpec((1, MLA_HEADS, LANES, tm), lambda bi, ti: (bi, 0, 0, ti)),
                   pl.BlockSpec((1, MLA_HEADS, tm, LANES), lambda bi, ti: (bi, 0, ti, 0)),
                   pl.BlockSpec((1, MLA_HEADS * (MLA_V + DV_ONES), tm), lambda bi, ti: (bi, 0, ti))],
        out_shape=[jax.ShapeDtypeStruct((b, MLA_HEADS, LANES, l), BF16),
                   jax.ShapeDtypeStruct((b, MLA_HEADS, l, LANES), BF16),
                   jax.ShapeDtypeStruct((b, MLA_HEADS * (MLA_V + DV_ONES), l), BF16)],
        compiler_params=_cparams(("parallel", "parallel")),
    )(h0, h0, h0, h0, pos3, *consts)


def _flash_init(m_ref, acc_ref):
    m_ref[...] = jnp.full_like(m_ref, MASKED)
    acc_ref[...] = jnp.zeros_like(acc_ref)


def _flash_update(h, s, vt, m_ref, acc_ref):
    dve = vt.shape[0]
    hr = slice(h, h + 1)
    rows = slice(h * dve, (h + 1) * dve)
    m_old = m_ref[hr, :]
    m_new = jnp.maximum(m_old, jnp.max(s, axis=0, keepdims=True))
    p = jnp.exp2(s - m_new)
    acc_ref[rows, :] = jnp.exp2(m_old - m_new) * acc_ref[rows, :] + _dot(vt, p.astype(BF16))
    m_ref[hr, :] = m_new


def _flash_finish(o_ref, acc_ref, heads, dv):
    dve = dv + DV_ONES
    outs = [acc_ref[h * dve:h * dve + dv, :] / acc_ref[h * dve + dv:h * dve + dv + 1, :] for h in range(heads)]
    o_ref[...] = jnp.concatenate(outs, axis=0).T.astype(o_ref.dtype)


def _values_with_ones(v, heads, dv):
    vt = v.T
    ones = jnp.ones((DV_ONES, v.shape[0]), F32)
    return jnp.concatenate([piece for h in range(heads) for piece in (vt[h * dv:(h + 1) * dv, :], ones)], axis=0)


def _mla_attn_kernel(qt_ref, k_ref, vt_ref, o_ref, m_ref, acc_ref, *, tk):
    tq = qt_ref.shape[3]
    q0 = pl.program_id(1) * tq
    nkb = (q0 + tq + tk - 1) // tk
    nfull = (q0 + 1) // tk
    key_i = lax.broadcasted_iota(jnp.int32, (tk, tq), 0)
    qry_i = q0 + lax.broadcasted_iota(jnp.int32, (tk, tq), 1)
    dve = MLA_V + DV_ONES
    _flash_init(m_ref, acc_ref)

    def block(kb, carry, masked):
        ks = pl.ds(pl.multiple_of(kb * tk, tk), tk)
        def scores(h):
            return _dot(k_ref[0, h, ks, :], qt_ref[0, h])

        ahead = [scores(h) for h in range(SCORES_AHEAD)]
        for h in range(MLA_HEADS):
            s = ahead.pop(0)
            if h + SCORES_AHEAD < MLA_HEADS:
                ahead.append(scores(h + SCORES_AHEAD))
            if masked:
                s = jnp.where(kb * tk + key_i <= qry_i, s, MASKED)
            _flash_update(h, s, vt_ref[0, h * dve:(h + 1) * dve, ks], m_ref, acc_ref)
        return carry

    lax.fori_loop(0, nfull, functools.partial(block, masked=False), 0)
    lax.fori_loop(nfull, nkb, functools.partial(block, masked=True), 0)
    _flash_finish(o_ref, acc_ref, MLA_HEADS, MLA_V)


def _mla_attn(qt, k, vt, b, l, tq, tk):
    nq = l // tq
    hv = MLA_HEADS * MLA_V
    hve = MLA_HEADS * (MLA_V + DV_ONES)
    return pl.pallas_call(
        functools.partial(_mla_attn_kernel, tk=tk),
        grid=(b, nq),
        in_specs=[pl.BlockSpec((1, MLA_HEADS, LANES, tq), lambda bi, i: (bi, 0, 0, i)),
                  pl.BlockSpec((1, MLA_HEADS, l, LANES), lambda bi, i: (bi, 0, 0, 0), pipeline_mode=pl.Buffered(1)),
                  pl.BlockSpec((1, hve, l), lambda bi, i: (bi, 0, 0), pipeline_mode=pl.Buffered(1))],
        out_specs=pl.BlockSpec((tq, hv), lambda bi, i: (bi * nq + i, 0)),
        out_shape=jax.ShapeDtypeStruct((b * l, hv), BF16),
        scratch_shapes=[pltpu.VMEM((MLA_HEADS, tq), F32), pltpu.VMEM((hve, tq), F32)],
        compiler_params=_cparams(("parallel", "arbitrary")),
    )(qt, k, vt)


def _out_ln_kernel(a_ref, b_ref, x_ref, w_ref, g_ref, bias_ref, o_ref):
    na = a_ref.shape[1]
    mix = _dot(a_ref[...], w_ref[:na, :]) + _dot(b_ref[...], w_ref[na:, :])
    o_ref[...] = _layernorm(ALPHA * x_ref[...] + mix, g_ref[...], bias_ref[...])


def _out_ln(a, bb, x2, w, g, bias, tm=512):
    t = x2.shape[0]
    na, nb = a.shape[1], bb.shape[1]
    return pl.pallas_call(
        _out_ln_kernel,
        grid=(t // tm,),
        in_specs=[pl.BlockSpec((tm, na), lambda i: (i, 0)),
                  pl.BlockSpec((tm, nb), lambda i: (i, 0)),
                  pl.BlockSpec((tm, D_MODEL), lambda i: (i, 0)),
                  pl.BlockSpec((na + nb, D_MODEL), lambda i: (0, 0)),
                  pl.BlockSpec((1, D_MODEL), lambda i: (0, 0)),
                  pl.BlockSpec((1, D_MODEL), lambda i: (0, 0))],
        out_specs=pl.BlockSpec((tm, D_MODEL), lambda i: (i, 0)),
        out_shape=jax.ShapeDtypeStruct((t, D_MODEL), F32),
        compiler_params=_cparams(("parallel",)),
    )(a, bb, x2, w, g, bias)


def _mlp_ln_kernel(x_ref, w1_ref, w2_ref, g_ref, bias_ref, o_ref, acc_ref):
    j = pl.program_id(1)

    @pl.when(j == 0)
    def _():
        acc_ref[...] = jnp.zeros_like(acc_ref)

    hid = jnp.maximum(_dot(x_ref[...].astype(BF16), w1_ref[...]), 0.0)
    acc_ref[...] += _dot((hid * hid).astype(BF16), w2_ref[...])

    @pl.when(j == pl.num_programs(1) - 1)
    def _():
        o_ref[...] = _layernorm(ALPHA * x_ref[...] + acc_ref[...], g_ref[...], bias_ref[...])


def _mlp_ln(x2, w1, w2, g, bias, tm=512, tf=1024):
    t = x2.shape[0]
    return pl.pallas_call(
        _mlp_ln_kernel,
        grid=(t // tm, D_FF // tf),
        in_specs=[pl.BlockSpec((tm, D_MODEL), lambda i, j: (i, 0)),
                  pl.BlockSpec((D_MODEL, tf), lambda i, j: (0, j)),
                  pl.BlockSpec((tf, D_MODEL), lambda i, j: (j, 0)),
                  pl.BlockSpec((1, D_MODEL), lambda i, j: (0, 0)),
                  pl.BlockSpec((1, D_MODEL), lambda i, j: (0, 0))],
        out_specs=pl.BlockSpec((tm, D_MODEL), lambda i, j: (i, 0)),
        out_shape=jax.ShapeDtypeStruct((t, D_MODEL), F32),
        scratch_shapes=[pltpu.VMEM((tm, D_MODEL), F32)],
        compiler_params=_cparams(("parallel", "arbitrary")),
    )(x2, w1, w2, g, bias)


def _proj1_kernel(x_ref, wb_ref, wf_ref, q_ref, qi_ref, k_ref, ki_ref, vt_ref, u_ref, wi_ref):
    xb = x_ref[...].astype(BF16)

    def cols(w_ref, start, width):
        return jnp.concatenate(
            [_dot(xb, w_ref[:, j:j + 256]) for j in range(start, start + width, 256)], axis=1)

    hd = DSA_HEADS * DSA_DH
    q_ref[...] = (cols(wb_ref, L1_Q, hd) * DSA_QSCALE).astype(BF16)
    qi_ref[...] = cols(wb_ref, L1_QI, hd).astype(BF16)
    k_ref[0] = cols(wb_ref, L1_K, hd).astype(BF16)
    ki_ref[0] = _dot(xb, wb_ref[:, L1_KI:L1_KI + LANES]).astype(BF16)
    vt_ref[0] = _values_with_ones(cols(wb_ref, L1_V, hd), DSA_HEADS, DSA_DH).astype(BF16)
    u_ref[...] = cols(wf_ref, L1_U, S5_W)
    wi_ref[...] = _dot(xb, wf_ref[:, L1_WI:L1_WI + LANES])


def _proj1(x2, wb, wf, b, l, tm=512):
    tm = min(tm, l)
    nt = l // tm
    t = b * l
    hd = DSA_HEADS * DSA_DH
    hde = DSA_HEADS * (DSA_DH + DV_ONES)
    flat = lambda bi, ti: (bi * nt + ti, 0)
    return pl.pallas_call(
        _proj1_kernel,
        grid=(b, nt),
        in_specs=[pl.BlockSpec((tm, D_MODEL), flat), _whole(wb, 2), _whole(wf, 2)],
        out_specs=[pl.BlockSpec((tm, hd), flat),
                   pl.BlockSpec((tm, hd), flat),
                   pl.BlockSpec((1, tm, hd), lambda bi, ti: (bi, ti, 0)),
                   pl.BlockSpec((1, tm, LANES), lambda bi, ti: (bi, ti, 0)),
                   pl.BlockSpec((1, hde, tm), lambda bi, ti: (bi, 0, ti)),
                   pl.BlockSpec((tm, S5_W), flat),
                   pl.BlockSpec((tm, LANES), flat)],
        out_shape=[jax.ShapeDtypeStruct((t, hd), BF16),
                   jax.ShapeDtypeStruct((t, hd), BF16),
                   jax.ShapeDtypeStruct((b, l, hd), BF16),
                   jax.ShapeDtypeStruct((b, l, LANES), BF16),
                   jax.ShapeDtypeStruct((b, hde, l), BF16),
                   jax.ShapeDtypeStruct((t, S5_W), F32),
                   jax.ShapeDtypeStruct((t, LANES), F32)],
        compiler_params=_cparams(("parallel", "parallel")),
    )(x2, wb, wf)


def _dsa_kernel(q_ref, qi_ref, wi_ref, k_ref, vt_ref, ki_ref, o_ref, key_ref, m_ref, acc_ref,
                *, topk, tk, seq):
    tq = q_ref.shape[0]
    q0 = pl.program_id(1) * tq
    nkb = (q0 + tq + tk - 1) // tk
    tc = math.gcd(tk, tq)
    nkc = (q0 + tq + tc - 1) // tc
    key_i = lax.broadcasted_iota(jnp.int32, (tk, tq), 0)
    qry_i = q0 + lax.broadcasted_iota(jnp.int32, (tk, tq), 1)
    qry_row = q0 + lax.broadcasted_iota(jnp.int32, (1, tq), 1)

    def kslice(kb):
        return pl.ds(pl.multiple_of(kb * tk, tk), tk)

    def head_rows(xt, h, width):
        per = LANES // width
        pair = xt[(h // per) * LANES:(h // per + 1) * LANES, :]
        r = lax.broadcasted_iota(jnp.int32, pair.shape, 0)
        lo = (h % per) * width
        return jnp.where((r >= lo) & (r < lo + width), pair, 0.0).astype(BF16)

    w_t = wi_ref[...].T[:IDX_HEADS, :] * (IDX_HEADS ** -0.5) * (IDX_DIM ** -0.5)
    qi_t = qi_ref[...].astype(F32).T
    zero_rows = jnp.zeros((LANES - IDX_DIM, tq), BF16)
    qi_heads = [jnp.concatenate([qi_t[h * IDX_DIM:(h + 1) * IDX_DIM, :].astype(BF16), zero_rows], axis=0)
                for h in range(IDX_HEADS)]

    def score_block(kb, carry):
        kib = ki_ref[0, kslice(kb), :]
        sc = jnp.zeros((tk, tq), F32)
        for h in range(IDX_HEADS):
            sc = sc + jnp.maximum(_dot(kib, qi_heads[h]), 0.0) * w_t[h:h + 1, :]
        sc = jnp.where(kb * tk + key_i <= qry_i, sc, -jnp.inf)
        bits = pltpu.bitcast(sc, jnp.int32)
        key_ref[kslice(kb), :] = jnp.where(bits < 0, bits ^ jnp.int32(0x7FFFFFFF), bits)
        return carry

    lax.fori_loop(0, nkb, score_block, 0)

    def count(pred):
        def body(kc, acc):
            keys = key_ref[pl.ds(pl.multiple_of(kc * tc, tc), tc), :]
            hit = jnp.where(pred(keys, kc * tc), 1, 0)
            return acc + jnp.sum(hit.reshape(tc // SUBLANES, SUBLANES, tq), axis=0)
        acc = lax.fori_loop(0, nkc, body, jnp.zeros((SUBLANES, tq), jnp.int32))
        return jnp.sum(acc, axis=0, keepdims=True)

    n_nonneg = count(lambda ks, k0: ks >= 0)
    prefix = jnp.where(n_nonneg >= topk, 0, INT_MIN).astype(jnp.int32)

    def bit_step(b, prefix):
        cand = prefix | jnp.left_shift(jnp.int32(1), 30 - b)
        n_ge = count(lambda ks, k0: ks >= cand)
        return jnp.where(n_ge >= topk, cand, prefix)

    thr = lax.fori_loop(0, 31, bit_step, prefix)
    n_gt = count(lambda ks, k0: ks > thr)
    n_eq = count(lambda ks, k0: ks == thr)
    need = topk - n_gt
    finite = thr != NEG_INF_KEY

    def tie_cut():
        nbits = max(1, (seq - 1).bit_length())
        key_c = lax.broadcasted_iota(jnp.int32, (tc, tq), 0)

        def step(b, ans):
            cand = ans | jnp.left_shift(jnp.int32(1), nbits - 1 - b)
            n_before = count(lambda ks, k0: (ks == thr) & (k0 + key_c < cand))
            return jnp.where(n_before < need, cand, ans)
        return lax.fori_loop(0, nbits, step, jnp.zeros((1, tq), jnp.int32))

    has_tie = jnp.max(jnp.where(finite & (n_eq > need), 1, 0)) > 0
    cut = lax.cond(has_tie, tie_cut, lambda: jnp.full((1, tq), seq, jnp.int32))
    cut = jnp.where(finite, cut, qry_row)

    def bias_block(kb, carry):
        keys = key_ref[kslice(kb), :]
        sel = (keys > thr) | ((keys == thr) & (kb * tk + key_i <= cut))
        key_ref[kslice(kb), :] = pltpu.bitcast(jnp.where(sel, jnp.inf, MASKED), jnp.int32)
        return carry

    lax.fori_loop(0, nkb, bias_block, 0)

    q_t = q_ref[...].astype(F32).T
    per = LANES // DSA_DH
    dve = DSA_DH + DV_ONES
    q_heads = [head_rows(q_t, h, DSA_DH) for h in range(DSA_HEADS)]
    _flash_init(m_ref, acc_ref)

    def attend(kb, carry):
        ks = kslice(kb)
        cap = pltpu.bitcast(key_ref[ks, :], F32)

        def scores(h):
            return _dot(k_ref[0, ks, (h // per) * LANES:(h // per + 1) * LANES], q_heads[h])

        ahead = [scores(h) for h in range(SCORES_AHEAD)]
        for h in range(DSA_HEADS):
            s = jnp.minimum(ahead.pop(0), cap)
            if h + SCORES_AHEAD < DSA_HEADS:
                ahead.append(scores(h + SCORES_AHEAD))
            _flash_update(h, s, vt_ref[0, h * dve:(h + 1) * dve, ks], m_ref, acc_ref)
        return carry

    lax.fori_loop(0, nkb, attend, 0)
    _flash_finish(o_ref, acc_ref, DSA_HEADS, DSA_DH)


def _dsa(q, qi, wi, k, vt, ki, b, l, tq, tk):
    nq = l // tq
    topk = min(TOPK_MAX, l // 4)
    hd = DSA_HEADS * DSA_DH
    hde = DSA_HEADS * (DSA_DH + DV_ONES)
    flat = lambda bi, i: (bi * nq + i, 0)
    batch = lambda bi, i: (bi, 0, 0)
    return pl.pallas_call(
        functools.partial(_dsa_kernel, topk=topk, tk=tk, seq=l),
        grid=(b, nq),
        in_specs=[pl.BlockSpec((tq, hd), flat),
                  pl.BlockSpec((tq, hd), flat),
                  pl.BlockSpec((tq, LANES), flat),
                  pl.BlockSpec((1, l, hd), batch, pipeline_mode=pl.Buffered(1)),
                  pl.BlockSpec((1, hde, l), batch, pipeline_mode=pl.Buffered(1)),
                  pl.BlockSpec((1, l, LANES), batch, pipeline_mode=pl.Buffered(1))],
        out_specs=pl.BlockSpec((tq, hd), flat),
        out_shape=jax.ShapeDtypeStruct((b * l, hd), BF16),
        scratch_shapes=[pltpu.VMEM((l, tq), jnp.int32), pltpu.VMEM((DSA_HEADS, tq), F32),
                        pltpu.VMEM((hde, tq), F32)],
        compiler_params=_cparams(("parallel", "arbitrary")),
    )(q, qi, wi, k, vt, ki)


def _s5_kernel(u_ref, wbr_ref, wbi_ref, pwr_ref, pwi_ref, wcr_ref, wci_ref, d_ref, gw_ref, gb_ref,
               o_ref, xr_ref, xi_ref, cr_ref, ci_ref):
    ts = u_ref.shape[0]

    @pl.when(pl.program_id(1) == 0)
    def _():
        cr_ref[...] = jnp.zeros_like(cr_ref)
        ci_ref[...] = jnp.zeros_like(ci_ref)

    u = u_ref[...]
    ub = u.astype(BF16)
    xr = _dot(ub, wbr_ref[...])
    xi = _dot(ub, wbi_ref[...])
    sub = lax.broadcasted_iota(jnp.int32, (ts, S5_N), 0) % SUBLANES
    for d in (1, 2, 4):
        ar = pwr_ref[d - 1:d, :]
        ai = pwi_ref[d - 1:d, :]
        sr = jnp.where(sub >= d, pltpu.roll(xr, d, axis=0), 0.0)
        si = jnp.where(sub >= d, pltpu.roll(xi, d, axis=0), 0.0)
        xr, xi = xr + (ar * sr - ai * si), xi + (ar * si + ai * sr)
    xr_ref[...] = xr
    xi_ref[...] = xi
    pwr = pwr_ref[...]
    pwi = pwi_ref[...]

    def group(g, carry):
        cr, ci = carry
        rs = pl.ds(pl.multiple_of(g * SUBLANES, SUBLANES), SUBLANES)
        gr = xr_ref[rs, :] + (pwr * cr - pwi * ci)
        gi = xi_ref[rs, :] + (pwr * ci + pwi * cr)
        xr_ref[rs, :] = gr
        xi_ref[rs, :] = gi
        return gr[SUBLANES - 1:SUBLANES, :], gi[SUBLANES - 1:SUBLANES, :]

    cr, ci = lax.fori_loop(0, ts // SUBLANES, group, (cr_ref[...], ci_ref[...]))
    cr_ref[...] = cr
    ci_ref[...] = ci
    y = _dot(xr_ref[...].astype(BF16), wcr_ref[...]) - _dot(xi_ref[...].astype(BF16), wci_ref[...])
    y = jax.nn.gelu(y + d_ref[...] * u)
    gate = jax.nn.sigmoid(_dot(y.astype(BF16), gw_ref[...]) + gb_ref[...])
    o_ref[...] = (y * gate).astype(o_ref.dtype)


def _s5(u, wbr, wbi, pwr, pwi, wcr, wci, d, gw, gb, b, l, ts=256):
    nt = l // ts
    consts = (wbr, wbi, pwr, pwi, wcr, wci, d, gw, gb)
    flat = lambda bi, ti: (bi * nt + ti, 0)
    return pl.pallas_call(
        _s5_kernel,
        grid=(b, nt),
        in_specs=[pl.BlockSpec((ts, S5_W), flat)] + [_whole(a, 2) for a in consts],
        out_specs=pl.BlockSpec((ts, S5_W), flat),
        out_shape=jax.ShapeDtypeStruct((b * l, S5_W), BF16),
        scratch_shapes=[pltpu.VMEM((ts, S5_N), F32), pltpu.VMEM((ts, S5_N), F32),
                        pltpu.VMEM((1, S5_N), F32), pltpu.VMEM((1, S5_N), F32)],
        compiler_params=_cparams(("parallel", "arbitrary")),
    )(u, *consts)


def _pad_cols(w, n):
    return jnp.pad(w, ((0, 0), (0, n - w.shape[1])))


def _l0_weights(w_in, w_uq, w_ukv):
    q, k, v, g_lr, r, c_q, c_kv, k_rope = jnp.split(
        w_in, [256, 512, 1024, 1040, 1552, 1808, 1936], axis=1)
    half = MLA_ROPE // 2
    t1, t2 = k_rope[:, :half], k_rope[:, half:]
    z = lambda n: jnp.zeros((w_in.shape[0], n), w_in.dtype)
    tail = LANES - MLA_NOPE - MLA_ROPE
    rope_a = jnp.concatenate([z(MLA_NOPE), t1, t2, z(tail)], axis=1)
    rope_b = jnp.concatenate([z(MLA_NOPE), -t2, t1, z(tail)], axis=1)
    w0 = jnp.concatenate([q, k, v, r, c_q, c_kv, _pad_cols(g_lr, LANES), rope_a, rope_b], axis=1)
    uq = w_uq.reshape(MLA_Q_RANK, MLA_HEADS, MLA_NOPE + MLA_ROPE)
    nope, r1, r2 = uq[..., :MLA_NOPE], uq[..., MLA_NOPE:MLA_NOPE + half], uq[..., MLA_NOPE + half:]
    zq = lambda n: jnp.zeros((MLA_Q_RANK, MLA_HEADS, n), w_uq.dtype)
    wqa = jnp.concatenate([nope, r1, r2, zq(tail)], axis=-1).reshape(MLA_Q_RANK, MLA_HEADS * LANES)
    wqb = jnp.concatenate([zq(MLA_NOPE), -r2, r1, zq(tail)], axis=-1).reshape(MLA_Q_RANK, MLA_HEADS * LANES)
    ukv = w_ukv.reshape(MLA_KV_RANK, MLA_HEADS, MLA_NOPE + MLA_V)
    wk = jnp.pad(ukv[..., :MLA_NOPE], ((0, 0), (0, 0), (0, LANES - MLA_NOPE))).reshape(MLA_KV_RANK, MLA_HEADS * LANES)
    wv = ukv[..., MLA_NOPE:].reshape(MLA_KV_RANK, MLA_HEADS * MLA_V)
    return w0.astype(BF16), wqa.astype(BF16), wqb.astype(BF16), wk.astype(BF16), wv.astype(BF16)


def _l1_weights(w_in):
    q, k, v, qi, ki, wi, u = jnp.split(w_in, [512, 1024, 1536, 2048, 2112, 2120], axis=1)
    wb = jnp.concatenate([q, qi, k, _pad_cols(ki, LANES), v], axis=1)
    wf = jnp.concatenate([u, _pad_cols(wi, LANES)], axis=1)
    return wb.astype(BF16), wf.astype(BF16)


def _s5_weights(a_re, a_im, b_re, b_im, c_re, c_im, log_step):
    lam_re = jnp.minimum(a_re, -1e-4)
    lam_im = a_im
    dt = jnp.exp(log_step)[:, None]
    mag = jnp.exp(lam_re * dt)
    abar_re = mag * jnp.cos(lam_im * dt)
    abar_im = mag * jnp.sin(lam_im * dt)
    den = jnp.square(lam_re) + jnp.square(lam_im)
    nr = abar_re - 1.0
    ni = abar_im
    coef_re = (nr * lam_re + ni * lam_im) / den
    coef_im = (ni * lam_re - nr * lam_im) / den
    bbar_re = coef_re[..., None] * b_re - coef_im[..., None] * b_im
    bbar_im = coef_re[..., None] * b_im + coef_im[..., None] * b_re
    eye = jnp.eye(S5_GROUPS, dtype=F32)
    blockdiag_in = lambda t: jnp.einsum('gpc,gh->gchp', t, eye).reshape(S5_W, S5_N).astype(BF16)
    blockdiag_out = lambda t: jnp.einsum('gcp,gh->gphc', t, eye).reshape(S5_N, S5_W).astype(BF16)
    pr, pi = [abar_re.reshape(1, S5_N)], [abar_im.reshape(1, S5_N)]
    for _ in range(SUBLANES - 1):
        pr, pi = (pr + [pr[-1] * pr[0] - pi[-1] * pi[0]], pi + [pr[-1] * pi[0] + pi[-1] * pr[0]])
    return (blockdiag_in(bbar_re), blockdiag_in(bbar_im), jnp.concatenate(pr, 0), jnp.concatenate(pi, 0),
            blockdiag_out(c_re), blockdiag_out(c_im))


def kernel(x, positions, l0_w_in, l0_gla_wg2, l0_gla_bg, l0_gla_norm, l0_mla_q_norm, l0_mla_w_uq, l0_mla_kv_norm, l0_mla_w_ukv, l0_w_out, l0_ln1_g, l0_ln1_b, l0_mlp_w1, l0_mlp_w2, l0_ln2_g, l0_ln2_b, l1_w_in, l1_s5_a_re, l1_s5_a_im, l1_s5_b_re, l1_s5_b_im, l1_s5_c_re, l1_s5_c_im, l1_s5_d, l1_s5_log_step, l1_glu_w, l1_glu_b, l1_w_out, l1_ln1_g, l1_ln1_b, l1_mlp_w1, l1_mlp_w2, l1_ln2_g, l1_ln2_b):
    b, l, _ = x.shape
    t = b * l
    row = lambda a: a.reshape(1, -1)
    x2 = x.reshape(t, D_MODEL)
    tq = min(256, l)
    tk = min(512, l)

    w0, wqa, wqb, wk, wv = _l0_weights(l0_w_in, l0_mla_w_uq, l0_mla_w_ukv)
    h0 = _proj(x2, w0, F32)
    o_gla = _gla(h0, l0_gla_wg2.astype(BF16), row(l0_gla_bg), row(l0_gla_norm), b, l)
    tm = min(512, l)
    half = MLA_ROPE // 2
    inv_freq = ROPE_THETA ** (-jnp.arange(half, dtype=F32) / half)
    invf = jnp.concatenate([inv_freq, inv_freq]).reshape(MLA_ROPE, 1)
    pos3 = positions.reshape(t // tm, 1, tm)
    qt, km, vt = _mla_proj(h0, pos3, invf, row(l0_mla_q_norm), row(l0_mla_kv_norm), wqa, wqb, wk, wv, b, l, tm)
    o_mla = _mla_attn(qt, km, vt, b, l, tq, tk)
    x2 = _out_ln(o_gla, o_mla, x2, l0_w_out.astype(BF16), row(l0_ln1_g), row(l0_ln1_b))
    x2 = _mlp_ln(x2, l0_mlp_w1.astype(BF16), l0_mlp_w2.astype(BF16), row(l0_ln2_g), row(l0_ln2_b))

    w1b, w1f = _l1_weights(l1_w_in)
    q1, qi1, k1, ki1, vt1, u1, wi1 = _proj1(x2, w1b, w1f, b, l)
    o_dsa = _dsa(q1, qi1, wi1, k1, vt1, ki1, b, l, tq, tk)
    s5w = _s5_weights(l1_s5_a_re, l1_s5_a_im, l1_s5_b_re, l1_s5_b_im, l1_s5_c_re, l1_s5_c_im, l1_s5_log_step)
    o_s5 = _s5(u1, *s5w, row(l1_s5_d), l1_glu_w.astype(BF16), row(l1_glu_b), b, l, min(256, l))
    x2 = _out_ln(o_dsa, o_s5, x2, l1_w_out.astype(BF16), row(l1_ln1_g), row(l1_ln1_b))
    x2 = _mlp_ln(x2, l1_mlp_w1.astype(BF16), l1_mlp_w2.astype(BF16), row(l1_ln2_g), row(l1_ln2_b))
    return x2.reshape(b, l, D_MODEL)
```

```python
import functools
import math

import jax
import jax.numpy as jnp
from jax import lax
from jax.experimental import pallas as pl
from jax.experimental.pallas import tpu as pltpu

BF16 = jnp.bfloat16
F32 = jnp.float32

D_MODEL = 1024
DEPTH = 2
GLA_HEADS, GLA_DK, GLA_DV, GLA_RANK, GLA_TAU, GLA_CHUNK = 4, 64, 128, 16, 16.0, 64
MLA_HEADS, MLA_Q_RANK, MLA_KV_RANK, MLA_NOPE, MLA_ROPE, MLA_V = 8, 256, 128, 64, 32, 64
ROPE_THETA = 10000.0
DSA_HEADS, DSA_DH, IDX_HEADS, IDX_DIM, TOPK_MAX = 8, 64, 8, 64, 256
S5_GROUPS, S5_CH, S5_STATE = 32, 16, 64
S5_W = S5_GROUPS * S5_CH
S5_TC = 16
D_FF = 4 * D_MODEL
LN_EPS = 1e-5
ALPHA = (2 * DEPTH) ** 0.25

LANES = 128
SUBLANES = 8
VMEM_LIMIT = 56 * 1024 * 1024
MASKED = -1e30
INT_MIN = -(2 ** 31)
NEG_INF_KEY = INT_MIN + 0x7FFFFF
LOG2E = math.log2(math.e)
SCORES_AHEAD = 3
DV_ONES = 16
MLA_QSCALE = (MLA_NOPE + MLA_ROPE) ** -0.5 * LOG2E
DSA_QSCALE = DSA_DH ** -0.5 * LOG2E

L0_Q, L0_K, L0_V, L0_R, L0_CQ, L0_CKV, L0_MISC, L0_ROPE_A, L0_ROPE_B, L0_N = (
    0, 256, 512, 1024, 1536, 1792, 1920, 2048, 2176, 2304)
L1_Q, L1_QI, L1_K, L1_KI, L1_V, L1_NB = 0, 512, 1024, 1536, 1664, 2176
L1_U, L1_WI, L1_NF = 0, 512, 640


def _cparams(sem):
    return pltpu.CompilerParams(dimension_semantics=sem, vmem_limit_bytes=VMEM_LIMIT)


def _dot(a, b):
    return jnp.dot(a, b, preferred_element_type=F32)


def _dot_nt(a, b):
    return lax.dot_general(a, b, (((1,), (1,)), ((), ())), preferred_element_type=F32)


def _layernorm(v, g, b):
    mu = jnp.mean(v, -1, keepdims=True)
    d = v - mu
    var = jnp.mean(d * d, -1, keepdims=True)
    return d * lax.rsqrt(var + LN_EPS) * g + b


def _rmsnorm(v, g):
    return v * lax.rsqrt(jnp.mean(v * v, -1, keepdims=True) + LN_EPS) * g


def _whole(a, nargs):
    return pl.BlockSpec(a.shape, lambda *_: (0,) * a.ndim)


def _proj_kernel(x_ref, w_ref, o_ref, *, chunk):
    xb = x_ref[...].astype(BF16)
    for j in range(0, o_ref.shape[1], chunk):
        o_ref[:, j:j + chunk] = _dot(xb, w_ref[:, j:j + chunk]).astype(o_ref.dtype)


def _proj(x2, w, out_dtype, tm=512):
    t, k = x2.shape
    n = w.shape[1]
    return pl.pallas_call(
        functools.partial(_proj_kernel, chunk=256),
        grid=(t // tm,),
        in_specs=[pl.BlockSpec((tm, k), lambda i: (i, 0)),
                  pl.BlockSpec((k, n), lambda i: (0, 0))],
        out_specs=pl.BlockSpec((tm, n), lambda i: (i, 0)),
        out_shape=jax.ShapeDtypeStruct((t, n), out_dtype),
        compiler_params=_cparams(("parallel",)),
    )(x2, w)


def _gla_kernel(q_ref, k_ref, v_ref, r_ref, misc_ref, wg2_ref, bg_ref, gn_ref, o_ref, st_ref):
    c = GLA_CHUNK
    rows = q_ref.shape[0]

    @pl.when(pl.program_id(1) == 0)
    def _():
        st_ref[...] = jnp.zeros_like(st_ref)

    z = _dot(misc_ref[:, :GLA_RANK].astype(BF16), wg2_ref[...]) + bg_ref[...]
    log_a = jax.nn.log_sigmoid(z) * (1.0 / GLA_TAU)
    row = lax.broadcasted_iota(jnp.int32, (rows, rows), 0)
    col = lax.broadcasted_iota(jnp.int32, (rows, rows), 1)
    tril = jnp.where((col <= row) & (col >= row - row % c), 1.0, 0.0).astype(BF16)
    a_hi = log_a.astype(BF16)
    rem = log_a - a_hi.astype(F32)
    a_mid = rem.astype(BF16)
    a_lo = (rem - a_mid.astype(F32)).astype(BF16)
    cum = _dot(tril, a_hi) + _dot(tril, a_mid) + _dot(tril, a_lo)
    k = k_ref[...]
    q_dec = (q_ref[...] * (GLA_DK ** -0.5) * jnp.exp(cum)).astype(BF16)
    k_inv = (k * jnp.exp(-cum)).astype(BF16)
    causal = (lax.broadcasted_iota(jnp.int32, (c, c), 1) <= lax.broadcasted_iota(jnp.int32, (c, c), 0))
    for j in range(rows // c):
        rs = slice(j * c, (j + 1) * c)
        cum_last = cum[(j + 1) * c - 1:(j + 1) * c, :]
        k_end = (k[rs, :] * jnp.exp(cum_last - cum[rs, :])).astype(BF16)
        dec = jnp.exp(cum_last)
        for h in range(GLA_HEADS):
            ks = slice(h * GLA_DK, (h + 1) * GLA_DK)
            vs = slice(h * GLA_DV, (h + 1) * GLA_DV)
            v = v_ref[rs, vs]
            att = jnp.where(causal, _dot_nt(q_dec[rs, ks], k_inv[rs, ks]), 0.0)
            st = st_ref[h]
            o = _dot(att.astype(BF16), v.astype(BF16)) + _dot_nt(q_dec[rs, ks], st.astype(BF16))
            st_ref[h] = st * dec[:, ks] + _dot(v.T.astype(BF16), k_end[:, ks])
            o = _rmsnorm(o, gn_ref[:, vs])
            r = r_ref[rs, vs]
            o_ref[rs, vs] = (o * (r * jax.nn.sigmoid(r))).astype(o_ref.dtype)


def _gla(h0, wg2, bg, gn, b, l, chunks_per_step=4):
    c = GLA_CHUNK * min(chunks_per_step, l // GLA_CHUNK)
    nc = l // c
    hk = GLA_HEADS * GLA_DK
    hv = GLA_HEADS * GLA_DV

    def rows(j):
        return lambda bi, ci: (bi * nc + ci, j)

    return pl.pallas_call(
        _gla_kernel,
        grid=(b, nc),
        in_specs=[pl.BlockSpec((c, hk), rows(L0_Q // hk)),
                  pl.BlockSpec((c, hk), rows(L0_K // hk)),
                  pl.BlockSpec((c, hv), rows(L0_V // hv)),
                  pl.BlockSpec((c, hv), rows(L0_R // hv)),
                  pl.BlockSpec((c, LANES), rows(L0_MISC // LANES)),
                  pl.BlockSpec((GLA_RANK, hk), lambda bi, ci: (0, 0)),
                  pl.BlockSpec((1, hk), lambda bi, ci: (0, 0)),
                  pl.BlockSpec((1, hv), lambda bi, ci: (0, 0))],
        out_specs=pl.BlockSpec((c, hv), rows(0)),
        out_shape=jax.ShapeDtypeStruct((b * l, hv), BF16),
        scratch_shapes=[pltpu.VMEM((GLA_HEADS, GLA_DV, GLA_DK), F32)],
        compiler_params=_cparams(("parallel", "arbitrary")),
    )(h0, h0, h0, h0, h0, wg2, bg, gn)


def _mla_proj_kernel(cq_ref, ckv_ref, ra_ref, rb_ref, pos_ref, invf_ref, qn_ref, kvn_ref,
                     wqa_ref, wqb_ref, wk_ref, wv_ref, qt_ref, k_ref, vt_ref):
    tm = cq_ref.shape[0]
    pos = pos_ref[0].astype(F32)
    ang = invf_ref[...] * pos
    ones = jnp.ones((MLA_NOPE, tm), F32)
    pad = jnp.zeros((LANES - MLA_NOPE - MLA_ROPE, tm), F32)
    cos = jnp.concatenate([ones, jnp.cos(ang), pad], axis=0).T
    sin = jnp.concatenate([jnp.zeros((MLA_NOPE, tm), F32), jnp.sin(ang), pad], axis=0).T
    cqn = _rmsnorm(cq_ref[...], qn_ref[...]).astype(BF16)
    ckvn = _rmsnorm(ckv_ref[...], kvn_ref[...]).astype(BF16)
    k_rot = ra_ref[...] * cos + rb_ref[...] * sin
    for h in range(MLA_HEADS):
        hs = slice(h * LANES, (h + 1) * LANES)
        qh = _dot(cqn, wqa_ref[:, hs]) * cos + _dot(cqn, wqb_ref[:, hs]) * sin
        qt_ref[0, h] = (qh * MLA_QSCALE).T.astype(qt_ref.dtype)
        k_ref[0, h] = (_dot(ckvn, wk_ref[:, hs]) + k_rot).astype(k_ref.dtype)
    vt_ref[0] = _values_with_ones(_dot(ckvn, wv_ref[...]), MLA_HEADS, MLA_V).astype(vt_ref.dtype)


def _mla_proj(h0, pos3, invf, qn, kvn, wqa, wqb, wk, wv, b, l, tm):
    nt = l // tm

    def rows(j):
        return lambda bi, ti: (bi * nt + ti, j)

    consts = (invf, qn, kvn, wqa, wqb, wk, wv)
    return pl.pallas_call(
        _mla_proj_kernel,
        grid=(b, nt),
        in_specs=[pl.BlockSpec((tm, MLA_Q_RANK), rows(L0_CQ // MLA_Q_RANK)),
                  pl.BlockSpec((tm, MLA_KV_RANK), rows(L0_CKV // MLA_KV_RANK)),
                  pl.BlockSpec((tm, LANES), rows(L0_ROPE_A // LANES)),
                  pl.BlockSpec((tm, LANES), rows(L0_ROPE_B // LANES)),
                  pl.BlockSpec((1, 1, tm), lambda bi, ti: (bi * nt + ti, 0, 0))]
                 + [_whole(a, 2) for a in consts],
        out_specs=[pl.BlockSpec((1, MLA_HEADS, LANES, tm), lambda bi, ti: (bi, 0, 0, ti)),
                   pl.BlockSpec((1, MLA_HEADS, tm, LANES), lambda bi, ti: (bi, 0, ti, 0)),
                   pl.BlockSpec((1, MLA_HEADS * (MLA_V + DV_ONES), tm), lambda bi, ti: (bi, 0, ti))],
        out_shape=[jax.ShapeDtypeStruct((b, MLA_HEADS, LANES, l), BF16),
                   jax.ShapeDtypeStruct((b, MLA_HEADS, l, LANES), BF16),
                   jax.ShapeDtypeStruct((b, MLA_HEADS * (MLA_V + DV_ONES), l), BF16)],
        compiler_params=_cparams(("parallel", "parallel")),
    )(h0, h0, h0, h0, pos3, *consts)


def _flash_init(m_ref, acc_ref):
    m_ref[...] = jnp.full_like(m_ref, MASKED)
    acc_ref[...] = jnp.zeros_like(acc_ref)


def _flash_update(h, s, vt, m_ref, acc_ref):
    dve = vt.shape[0]
    hr = slice(h, h + 1)
    rows = slice(h * dve, (h + 1) * dve)
    m_old = m_ref[hr, :]
    m_new = jnp.maximum(m_old, jnp.max(s, axis=0, keepdims=True))
    p = jnp.exp2(s - m_new)
    acc_ref[rows, :] = jnp.exp2(m_old - m_new) * acc_ref[rows, :] + _dot(vt, p.astype(BF16))
    m_ref[hr, :] = m_new


def _flash_finish(o_ref, acc_ref, heads, dv):
    dve = dv + DV_ONES
    outs = [acc_ref[h * dve:h * dve + dv, :] / acc_ref[h * dve + dv:h * dve + dv + 1, :] for h in range(heads)]
    o_ref[...] = jnp.concatenate(outs, axis=0).T.astype(o_ref.dtype)


def _values_with_ones(v, heads, dv):
    vt = v.T
    ones = jnp.ones((DV_ONES, v.shape[0]), F32)
    return jnp.concatenate([piece for h in range(heads) for piece in (vt[h * dv:(h + 1) * dv, :], ones)], axis=0)


def _mla_attn_kernel(qt_ref, k_ref, vt_ref, o_ref, m_ref, acc_ref, *, tk):
    tq = qt_ref.shape[3]
    q0 = pl.program_id(1) * tq
    nkb = (q0 + tq + tk - 1) // tk
    nfull = (q0 + 1) // tk
    key_i = lax.broadcasted_iota(jnp.int32, (tk, tq), 0)
    qry_i = q0 + lax.broadcasted_iota(jnp.int32, (tk, tq), 1)
    dve = MLA_V + DV_ONES
    _flash_init(m_ref, acc_ref)

    def block(kb, carry, masked):
        ks = pl.ds(pl.multiple_of(kb * tk, tk), tk)
        def scores(h):
            return _dot(k_ref[0, h, ks, :], qt_ref[0, h])

        ahead = [scores(h) for h in range(SCORES_AHEAD)]
        for h in range(MLA_HEADS):
            s = ahead.pop(0)
            if h + SCORES_AHEAD < MLA_HEADS:
                ahead.append(scores(h + SCORES_AHEAD))
            if masked:
                s = jnp.where(kb * tk + key_i <= qry_i, s, MASKED)
            _flash_update(h, s, vt_ref[0, h * dve:(h + 1) * dve, ks], m_ref, acc_ref)
        return carry

    lax.fori_loop(0, nfull, functools.partial(block, masked=False), 0)
    lax.fori_loop(nfull, nkb, functools.partial(block, masked=True), 0)
    _flash_finish(o_ref, acc_ref, MLA_HEADS, MLA_V)


def _mla_attn(qt, k, vt, b, l, tq, tk):
    nq = l // tq
    hv = MLA_HEADS * MLA_V
    hve = MLA_HEADS * (MLA_V + DV_ONES)
    return pl.pallas_call(
        functools.partial(_mla_attn_kernel, tk=tk),
        grid=(b, nq),
        in_specs=[pl.BlockSpec((1, MLA_HEADS, LANES, tq), lambda bi, i: (bi, 0, 0, i)),
                  pl.BlockSpec((1, MLA_HEADS, l, LANES), lambda bi, i: (bi, 0, 0, 0), pipeline_mode=pl.Buffered(1)),
                  pl.BlockSpec((1, hve, l), lambda bi, i: (bi, 0, 0), pipeline_mode=pl.Buffered(1))],
        out_specs=pl.BlockSpec((tq, hv), lambda bi, i: (bi * nq + i, 0)),
        out_shape=jax.ShapeDtypeStruct((b * l, hv), BF16),
        scratch_shapes=[pltpu.VMEM((MLA_HEADS, tq), F32), pltpu.VMEM((hve, tq), F32)],
        compiler_params=_cparams(("parallel", "arbitrary")),
    )(qt, k, vt)


def _out_ln_kernel(a_ref, b_ref, x_ref, w_ref, g_ref, bias_ref, o_ref):
    na = a_ref.shape[1]
    mix = _dot(a_ref[...], w_ref[:na, :]) + _dot(b_ref[...], w_ref[na:, :])
    o_ref[...] = _layernorm(ALPHA * x_ref[...] + mix, g_ref[...], bias_ref[...])


def _out_ln(a, bb, x2, w, g, bias, tm=512):
    t = x2.shape[0]
    na, nb = a.shape[1], bb.shape[1]
    return pl.pallas_call(
        _out_ln_kernel,
        grid=(t // tm,),
        in_specs=[pl.BlockSpec((tm, na), lambda i: (i, 0)),
                  pl.BlockSpec((tm, nb), lambda i: (i, 0)),
                  pl.BlockSpec((tm, D_MODEL), lambda i: (i, 0)),
                  pl.BlockSpec((na + nb, D_MODEL), lambda i: (0, 0)),
                  pl.BlockSpec((1, D_MODEL), lambda i: (0, 0)),
                  pl.BlockSpec((1, D_MODEL), lambda i: (0, 0))],
        out_specs=pl.BlockSpec((tm, D_MODEL), lambda i: (i, 0)),
        out_shape=jax.ShapeDtypeStruct((t, D_MODEL), F32),
        compiler_params=_cparams(("parallel",)),
    )(a, bb, x2, w, g, bias)


def _mlp_ln_kernel(x_ref, w1_ref, w2_ref, g_ref, bias_ref, o_ref, acc_ref):
    j = pl.program_id(1)

    @pl.when(j == 0)
    def _():
        acc_ref[...] = jnp.zeros_like(acc_ref)

    hid = jnp.maximum(_dot(x_ref[...].astype(BF16), w1_ref[...]), 0.0)
    acc_ref[...] += _dot((hid * hid).astype(BF16), w2_ref[...])

    @pl.when(j == pl.num_programs(1) - 1)
    def _():
        o_ref[...] = _layernorm(ALPHA * x_ref[...] + acc_ref[...], g_ref[...], bias_ref[...])


def _mlp_ln(x2, w1, w2, g, bias, tm=1024, tf=1024):
    t = x2.shape[0]
    return pl.pallas_call(
        _mlp_ln_kernel,
        grid=(t // tm, D_FF // tf),
        in_specs=[pl.BlockSpec((tm, D_MODEL), lambda i, j: (i, 0)),
                  pl.BlockSpec((D_MODEL, tf), lambda i, j: (0, j)),
                  pl.BlockSpec((tf, D_MODEL), lambda i, j: (j, 0)),
                  pl.BlockSpec((1, D_MODEL), lambda i, j: (0, 0)),
                  pl.BlockSpec((1, D_MODEL), lambda i, j: (0, 0))],
        out_specs=pl.BlockSpec((tm, D_MODEL), lambda i, j: (i, 0)),
        out_shape=jax.ShapeDtypeStruct((t, D_MODEL), F32),
        scratch_shapes=[pltpu.VMEM((tm, D_MODEL), F32)],
        compiler_params=_cparams(("parallel", "arbitrary")),
    )(x2, w1, w2, g, bias)


def _proj1_kernel(x_ref, wb_ref, wf_ref, q_ref, qi_ref, k_ref, ki_ref, vt_ref, u_ref, wi_ref):
    xb = x_ref[...].astype(BF16)

    def cols(w_ref, start, width):
        return jnp.concatenate(
            [_dot(xb, w_ref[:, j:j + 256]) for j in range(start, start + width, 256)], axis=1)

    hd = DSA_HEADS * DSA_DH
    q_ref[...] = (cols(wb_ref, L1_Q, hd) * DSA_QSCALE).astype(BF16)
    qi_ref[...] = cols(wb_ref, L1_QI, hd).astype(BF16)
    k_ref[0] = cols(wb_ref, L1_K, hd).astype(BF16)
    ki_ref[0] = _dot(xb, wb_ref[:, L1_KI:L1_KI + LANES]).astype(BF16)
    vt_ref[0] = _values_with_ones(cols(wb_ref, L1_V, hd), DSA_HEADS, DSA_DH).astype(BF16)
    u_ref[...] = cols(wf_ref, L1_U, S5_W)
    wi_ref[...] = _dot(xb, wf_ref[:, L1_WI:L1_WI + LANES])


def _proj1(x2, wb, wf, b, l, tm=512):
    tm = min(tm, l)
    nt = l // tm
    t = b * l
    hd = DSA_HEADS * DSA_DH
    hde = DSA_HEADS * (DSA_DH + DV_ONES)
    flat = lambda bi, ti: (bi * nt + ti, 0)
    return pl.pallas_call(
        _proj1_kernel,
        grid=(b, nt),
        in_specs=[pl.BlockSpec((tm, D_MODEL), flat), _whole(wb, 2), _whole(wf, 2)],
        out_specs=[pl.BlockSpec((tm, hd), flat),
                   pl.BlockSpec((tm, hd), flat),
                   pl.BlockSpec((1, tm, hd), lambda bi, ti: (bi, ti, 0)),
                   pl.BlockSpec((1, tm, LANES), lambda bi, ti: (bi, ti, 0)),
                   pl.BlockSpec((1, hde, tm), lambda bi, ti: (bi, 0, ti)),
                   pl.BlockSpec((tm, S5_W), flat),
                   pl.BlockSpec((tm, LANES), flat)],
        out_shape=[jax.ShapeDtypeStruct((t, hd), BF16),
                   jax.ShapeDtypeStruct((t, hd), BF16),
                   jax.ShapeDtypeStruct((b, l, hd), BF16),
                   jax.ShapeDtypeStruct((b, l, LANES), BF16),
                   jax.ShapeDtypeStruct((b, hde, l), BF16),
                   jax.ShapeDtypeStruct((t, S5_W), F32),
                   jax.ShapeDtypeStruct((t, LANES), F32)],
        compiler_params=_cparams(("parallel", "parallel")),
    )(x2, wb, wf)


def _dsa_kernel(q_ref, qi_ref, wi_ref, k_ref, vt_ref, ki_ref, o_ref, key_ref, m_ref, acc_ref,
                *, topk, tk, seq):
    tq = q_ref.shape[0]
    q0 = pl.program_id(1) * tq
    nkb = (q0 + tq + tk - 1) // tk
    tc = math.gcd(tk, tq)
    nkc = (q0 + tq + tc - 1) // tc
    key_i = lax.broadcasted_iota(jnp.int32, (tk, tq), 0)
    qry_i = q0 + lax.broadcasted_iota(jnp.int32, (tk, tq), 1)
    qry_row = q0 + lax.broadcasted_iota(jnp.int32, (1, tq), 1)

    def kslice(kb):
        return pl.ds(pl.multiple_of(kb * tk, tk), tk)

    def head_rows(xt, h, width):
        per = LANES // width
        pair = xt[(h // per) * LANES:(h // per + 1) * LANES, :]
        r = lax.broadcasted_iota(jnp.int32, pair.shape, 0)
        lo = (h % per) * width
        return jnp.where((r >= lo) & (r < lo + width), pair, 0.0).astype(BF16)

    w_t = wi_ref[...].T[:IDX_HEADS, :] * (IDX_HEADS ** -0.5) * (IDX_DIM ** -0.5)
    qi_t = qi_ref[...].astype(F32).T
    zero_rows = jnp.zeros((LANES - IDX_DIM, tq), BF16)
    qi_heads = [jnp.concatenate([qi_t[h * IDX_DIM:(h + 1) * IDX_DIM, :].astype(BF16), zero_rows], axis=0)
                for h in range(IDX_HEADS)]

    def score_block(kb, carry):
        kib = ki_ref[0, kslice(kb), :]
        sc = jnp.zeros((tk, tq), F32)
        for h in range(IDX_HEADS):
            sc = sc + jnp.maximum(_dot(kib, qi_heads[h]), 0.0) * w_t[h:h + 1, :]
        sc = jnp.where(kb * tk + key_i <= qry_i, sc, -jnp.inf)
        bits = pltpu.bitcast(sc, jnp.int32)
        key_ref[kslice(kb), :] = jnp.where(bits < 0, bits ^ jnp.int32(0x7FFFFFFF), bits)
        return carry

    lax.fori_loop(0, nkb, score_block, 0)

    def count(pred):
        def body(kc, acc):
            keys = key_ref[pl.ds(pl.multiple_of(kc * tc, tc), tc), :]
            hit = jnp.where(pred(keys, kc * tc), 1, 0)
            return acc + jnp.sum(hit.reshape(tc // SUBLANES, SUBLANES, tq), axis=0)
        acc = lax.fori_loop(0, nkc, body, jnp.zeros((SUBLANES, tq), jnp.int32))
        return jnp.sum(acc, axis=0, keepdims=True)

    n_nonneg = count(lambda ks, k0: ks >= 0)
    nonneg = n_nonneg >= topk
    start = (jnp.where(nonneg, 0, INT_MIN).astype(jnp.int32), jnp.where(nonneg, n_nonneg, nkc * tc))

    def bit_step(b, carry):
        prefix, n_prefix = carry
        cand = prefix | jnp.left_shift(jnp.int32(1), 30 - b)
        n_ge = count(lambda ks, k0: ks >= cand)
        take = n_ge >= topk
        return jnp.where(take, cand, prefix), jnp.where(take, n_ge, n_prefix)

    thr, n_ge_thr = lax.fori_loop(0, 31, bit_step, start)
    finite = thr != NEG_INF_KEY

    def tie_cut():
        need = topk - count(lambda ks, k0: ks > thr)
        nbits = max(1, (seq - 1).bit_length())
        key_c = lax.broadcasted_iota(jnp.int32, (tc, tq), 0)

        def step(b, ans):
            cand = ans | jnp.left_shift(jnp.int32(1), nbits - 1 - b)
            n_before = count(lambda ks, k0: (ks == thr) & (k0 + key_c < cand))
            return jnp.where(n_before < need, cand, ans)
        return lax.fori_loop(0, nbits, step, jnp.zeros((1, tq), jnp.int32))

    has_tie = jnp.max(jnp.where(finite & (n_ge_thr > topk), 1, 0)) > 0
    cut = lax.cond(has_tie, tie_cut, lambda: jnp.full((1, tq), seq, jnp.int32))
    cut = jnp.where(finite, cut, qry_row)

    def bias_block(kb, carry):
        keys = key_ref[kslice(kb), :]
        sel = (keys > thr) | ((keys == thr) & (kb * tk + key_i <= cut))
        key_ref[kslice(kb), :] = pltpu.bitcast(jnp.where(sel, jnp.inf, MASKED), jnp.int32)
        return carry

    lax.fori_loop(0, nkb, bias_block, 0)

    q_t = q_ref[...].astype(F32).T
    per = LANES // DSA_DH
    dve = DSA_DH + DV_ONES
    q_heads = [head_rows(q_t, h, DSA_DH) for h in range(DSA_HEADS)]
    _flash_init(m_ref, acc_ref)

    def attend(kb, carry):
        ks = kslice(kb)
        cap = pltpu.bitcast(key_ref[ks, :], F32)

        def scores(h):
            return _dot(k_ref[0, ks, (h // per) * LANES:(h // per + 1) * LANES], q_heads[h])

        ahead = [scores(h) for h in range(SCORES_AHEAD)]
        for h in range(DSA_HEADS):
            s = jnp.minimum(ahead.pop(0), cap)
            if h + SCORES_AHEAD < DSA_HEADS:
                ahead.append(scores(h + SCORES_AHEAD))
            _flash_update(h, s, vt_ref[0, h * dve:(h + 1) * dve, ks], m_ref, acc_ref)
        return carry

    lax.fori_loop(0, nkb, attend, 0)
    _flash_finish(o_ref, acc_ref, DSA_HEADS, DSA_DH)


def _dsa(q, qi, wi, k, vt, ki, b, l, tq, tk):
    nq = l // tq
    topk = min(TOPK_MAX, l // 4)
    hd = DSA_HEADS * DSA_DH
    hde = DSA_HEADS * (DSA_DH + DV_ONES)
    flat = lambda bi, i: (bi * nq + i, 0)
    batch = lambda bi, i: (bi, 0, 0)
    return pl.pallas_call(
        functools.partial(_dsa_kernel, topk=topk, tk=tk, seq=l),
        grid=(b, nq),
        in_specs=[pl.BlockSpec((tq, hd), flat),
                  pl.BlockSpec((tq, hd), flat),
                  pl.BlockSpec((tq, LANES), flat),
                  pl.BlockSpec((1, l, hd), batch, pipeline_mode=pl.Buffered(1)),
                  pl.BlockSpec((1, hde, l), batch, pipeline_mode=pl.Buffered(1)),
                  pl.BlockSpec((1, l, LANES), batch, pipeline_mode=pl.Buffered(1))],
        out_specs=pl.BlockSpec((tq, hd), flat),
        out_shape=jax.ShapeDtypeStruct((b * l, hd), BF16),
        scratch_shapes=[pltpu.VMEM((l, tq), jnp.int32), pltpu.VMEM((DSA_HEADS, tq), F32),
                        pltpu.VMEM((hde, tq), F32)],
        compiler_params=_cparams(("parallel", "arbitrary")),
    )(q, qi, wi, k, vt, ki)


def _s5_kernel(u_ref, m1_ref, m2_ref, m3_ref, pa_ref, pb_ref, y_ref, *, seg):
    u = u_ref[0]
    x = _dot(u, m2_ref[0])
    pos = lax.broadcasted_iota(jnp.int32, x.shape, 0) % seg
    d, k = 1, 0
    while d < seg:
        sh = jnp.where(pos >= d, pltpu.roll(x, d, axis=0), 0.0)
        x = x + sh * pa_ref[0, k:k + 1, :] + pltpu.roll(sh, S5_STATE, axis=1) * pb_ref[0, k:k + 1, :]
        d, k = 2 * d, k + 1
    x_in = jnp.where(pos >= 1, pltpu.roll(x, 1, axis=0), 0.0)
    y_ref[0] = _dot(u, m1_ref[0]) + _dot(x_in.astype(BF16), m3_ref[0])


def _s5(u_c, m1, m2, m3, pa, pb, seg):
    g, rows, cw = u_c.shape
    per_group = lambda a: pl.BlockSpec((1,) + a.shape[1:], lambda gi: (gi,) + (0,) * (a.ndim - 1))
    return pl.pallas_call(
        functools.partial(_s5_kernel, seg=seg),
        grid=(g,),
        in_specs=[per_group(a) for a in (u_c, m1, m2, m3, pa, pb)],
        out_specs=pl.BlockSpec((1, rows, cw), lambda gi: (gi, 0, 0)),
        out_shape=jax.ShapeDtypeStruct((g, rows, cw), F32),
        compiler_params=_cparams(("parallel",)),
    )(u_c, m1, m2, m3, pa, pb)


def _s5_out_kernel(y_ref, u_ref, d_ref, gw_ref, gb_ref, o_ref):
    y = jax.nn.gelu(y_ref[...] + d_ref[...] * u_ref[...])
    gate = jax.nn.sigmoid(_dot(y.astype(BF16), gw_ref[...]) + gb_ref[...])
    o_ref[...] = (y * gate).astype(o_ref.dtype)


def _s5_out(y, u, d, gw, gb, tm=1024):
    t = y.shape[0]
    tm = min(tm, t)
    return pl.pallas_call(
        _s5_out_kernel,
        grid=(t // tm,),
        in_specs=[pl.BlockSpec((tm, S5_W), lambda i: (i, 0)),
                  pl.BlockSpec((tm, S5_W), lambda i: (i, 0)),
                  _whole(d, 1), _whole(gw, 1), _whole(gb, 1)],
        out_specs=pl.BlockSpec((tm, S5_W), lambda i: (i, 0)),
        out_shape=jax.ShapeDtypeStruct((t, S5_W), BF16),
        compiler_params=_cparams(("parallel",)),
    )(y, u, d, gw, gb)


def _pad_cols(w, n):
    return jnp.pad(w, ((0, 0), (0, n - w.shape[1])))


def _l0_weights(w_in, w_uq, w_ukv):
    q, k, v, g_lr, r, c_q, c_kv, k_rope = jnp.split(
        w_in, [256, 512, 1024, 1040, 1552, 1808, 1936], axis=1)
    half = MLA_ROPE // 2
    t1, t2 = k_rope[:, :half], k_rope[:, half:]
    z = lambda n: jnp.zeros((w_in.shape[0], n), w_in.dtype)
    tail = LANES - MLA_NOPE - MLA_ROPE
    rope_a = jnp.concatenate([z(MLA_NOPE), t1, t2, z(tail)], axis=1)
    rope_b = jnp.concatenate([z(MLA_NOPE), -t2, t1, z(tail)], axis=1)
    w0 = jnp.concatenate([q, k, v, r, c_q, c_kv, _pad_cols(g_lr, LANES), rope_a, rope_b], axis=1)
    uq = w_uq.reshape(MLA_Q_RANK, MLA_HEADS, MLA_NOPE + MLA_ROPE)
    nope, r1, r2 = uq[..., :MLA_NOPE], uq[..., MLA_NOPE:MLA_NOPE + half], uq[..., MLA_NOPE + half:]
    zq = lambda n: jnp.zeros((MLA_Q_RANK, MLA_HEADS, n), w_uq.dtype)
    wqa = jnp.concatenate([nope, r1, r2, zq(tail)], axis=-1).reshape(MLA_Q_RANK, MLA_HEADS * LANES)
    wqb = jnp.concatenate([zq(MLA_NOPE), -r2, r1, zq(tail)], axis=-1).reshape(MLA_Q_RANK, MLA_HEADS * LANES)
    ukv = w_ukv.reshape(MLA_KV_RANK, MLA_HEADS, MLA_NOPE + MLA_V)
    wk = jnp.pad(ukv[..., :MLA_NOPE], ((0, 0), (0, 0), (0, LANES - MLA_NOPE))).reshape(MLA_KV_RANK, MLA_HEADS * LANES)
    wv = ukv[..., MLA_NOPE:].reshape(MLA_KV_RANK, MLA_HEADS * MLA_V)
    return w0.astype(BF16), wqa.astype(BF16), wqb.astype(BF16), wk.astype(BF16), wv.astype(BF16)


def _l1_weights(w_in):
    q, k, v, qi, ki, wi, u = jnp.split(w_in, [512, 1024, 1536, 2048, 2112, 2120], axis=1)
    wb = jnp.concatenate([q, qi, k, _pad_cols(ki, LANES), v], axis=1)
    wf = jnp.concatenate([u, _pad_cols(wi, LANES)], axis=1)
    return wb.astype(BF16), wf.astype(BF16)


def _s5_weights(a_re, a_im, b_re, b_im, c_re, c_im, log_step, nsteps):
    lam_re = jnp.minimum(a_re, -1e-4)
    lam_im = a_im
    dt = jnp.exp(log_step)[:, None]
    mag = jnp.exp(lam_re * dt)
    abar_re = mag * jnp.cos(lam_im * dt)
    abar_im = mag * jnp.sin(lam_im * dt)
    den = jnp.square(lam_re) + jnp.square(lam_im)
    nr = abar_re - 1.0
    ni = abar_im
    coef_re = (nr * lam_re + ni * lam_im) / den
    coef_im = (ni * lam_re - nr * lam_im) / den
    bbar_re = coef_re[..., None] * b_re - coef_im[..., None] * b_im
    bbar_im = coef_re[..., None] * b_im + coef_im[..., None] * b_re
    tc, g, cw = S5_TC, S5_GROUPS, S5_TC * S5_CH
    pr, pi = [jnp.ones_like(abar_re)], [jnp.zeros_like(abar_im)]
    for _ in range(tc):
        pr, pi = (pr + [pr[-1] * abar_re - pi[-1] * abar_im], pi + [pr[-1] * abar_im + pi[-1] * abar_re])
    pr, pi = jnp.stack(pr), jnp.stack(pi)
    ab_re = pr[:tc, :, :, None] * bbar_re - pi[:tc, :, :, None] * bbar_im
    ab_im = pr[:tc, :, :, None] * bbar_im + pi[:tc, :, :, None] * bbar_re
    taps = jnp.einsum('gop,dgpi->dgoi', c_re, ab_re) - jnp.einsum('gop,dgpi->dgoi', c_im, ab_im)
    lag = jnp.arange(tc)[None, :] - jnp.arange(tc)[:, None]
    m1 = jnp.where((lag >= 0)[:, :, None, None, None], taps[jnp.clip(lag, 0, tc - 1)], 0.0)
    m1 = m1.transpose(2, 0, 4, 1, 3).reshape(g, cw, cw)
    m2 = jnp.concatenate([ab_re[::-1], ab_im[::-1]], axis=2)
    m2 = m2.transpose(1, 0, 3, 2).reshape(g, cw, 2 * S5_STATE)
    d_re = c_re[None] * pr[1:, :, None, :] - c_im[None] * pi[1:, :, None, :]
    d_im = c_re[None] * pi[1:, :, None, :] + c_im[None] * pr[1:, :, None, :]
    m3 = jnp.concatenate([d_re, -d_im], axis=3).transpose(1, 3, 0, 2).reshape(g, 2 * S5_STATE, cw)
    qr, qi, pa, pb = pr[tc], pi[tc], [], []
    for _ in range(nsteps):
        pa.append(jnp.concatenate([qr, qr], axis=-1))
        pb.append(jnp.concatenate([-qi, qi], axis=-1))
        qr, qi = qr * qr - qi * qi, 2.0 * qr * qi
    return m1.astype(BF16), m2.astype(BF16), m3.astype(BF16), jnp.stack(pa, axis=1), jnp.stack(pb, axis=1)


def kernel(x, positions, l0_w_in, l0_gla_wg2, l0_gla_bg, l0_gla_norm, l0_mla_q_norm, l0_mla_w_uq, l0_mla_kv_norm, l0_mla_w_ukv, l0_w_out, l0_ln1_g, l0_ln1_b, l0_mlp_w1, l0_mlp_w2, l0_ln2_g, l0_ln2_b, l1_w_in, l1_s5_a_re, l1_s5_a_im, l1_s5_b_re, l1_s5_b_im, l1_s5_c_re, l1_s5_c_im, l1_s5_d, l1_s5_log_step, l1_glu_w, l1_glu_b, l1_w_out, l1_ln1_g, l1_ln1_b, l1_mlp_w1, l1_mlp_w2, l1_ln2_g, l1_ln2_b):
    b, l, _ = x.shape
    t = b * l
    row = lambda a: a.reshape(1, -1)
    x2 = x.reshape(t, D_MODEL)
    tq = min(256, l)
    tk = min(512, l)

    w0, wqa, wqb, wk, wv = _l0_weights(l0_w_in, l0_mla_w_uq, l0_mla_w_ukv)
    h0 = _proj(x2, w0, F32)
    o_gla = _gla(h0, l0_gla_wg2.astype(BF16), row(l0_gla_bg), row(l0_gla_norm), b, l)
    tm = min(512, l)
    half = MLA_ROPE // 2
    inv_freq = ROPE_THETA ** (-jnp.arange(half, dtype=F32) / half)
    invf = jnp.concatenate([inv_freq, inv_freq]).reshape(MLA_ROPE, 1)
    pos3 = positions.reshape(t // tm, 1, tm)
    qt, km, vt = _mla_proj(h0, pos3, invf, row(l0_mla_q_norm), row(l0_mla_kv_norm), wqa, wqb, wk, wv, b, l, tm)
    o_mla = _mla_attn(qt, km, vt, b, l, tq, tk)
    x2 = _out_ln(o_gla, o_mla, x2, l0_w_out.astype(BF16), row(l0_ln1_g), row(l0_ln1_b))
    x2 = _mlp_ln(x2, l0_mlp_w1.astype(BF16), l0_mlp_w2.astype(BF16), row(l0_ln2_g), row(l0_ln2_b))

    w1b, w1f = _l1_weights(l1_w_in)
    q1, qi1, k1, ki1, vt1, u1, wi1 = _proj1(x2, w1b, w1f, b, l)
    o_dsa = _dsa(q1, qi1, wi1, k1, vt1, ki1, b, l, tq, tk)
    seg = l // S5_TC
    s5w = _s5_weights(l1_s5_a_re, l1_s5_a_im, l1_s5_b_re, l1_s5_b_im, l1_s5_c_re, l1_s5_c_im, l1_s5_log_step,
                      max(1, (seg - 1).bit_length()))
    u_c = u1.astype(BF16).reshape(b, seg, S5_TC, S5_GROUPS, S5_CH).transpose(3, 0, 1, 2, 4)
    y_c = _s5(u_c.reshape(S5_GROUPS, b * seg, S5_TC * S5_CH), *s5w, seg)
    y1 = y_c.reshape(S5_GROUPS, b, seg, S5_TC, S5_CH).transpose(1, 2, 3, 0, 4).reshape(t, S5_W)
    o_s5 = _s5_out(y1, u1, row(l1_s5_d), l1_glu_w.astype(BF16), row(l1_glu_b))
    x2 = _out_ln(o_dsa, o_s5, x2, l1_w_out.astype(BF16), row(l1_ln1_g), row(l1_ln1_b))
    x2 = _mlp_ln(x2, l1_mlp_w1.astype(BF16), l1_mlp_w2.astype(BF16), row(l1_ln2_g), row(l1_ln2_b))
    return x2.reshape(b, l, D_MODEL)
```

```python
import functools
import math

import jax
import jax.numpy as jnp
from jax import lax
from jax.experimental import pallas as pl
from jax.experimental.pallas import tpu as pltpu

BF16 = jnp.bfloat16
F32 = jnp.float32

D_MODEL = 1024
DEPTH = 2
GLA_HEADS, GLA_DK, GLA_DV, GLA_RANK, GLA_TAU, GLA_CHUNK = 4, 64, 128, 16, 16.0, 64
MLA_HEADS, MLA_Q_RANK, MLA_KV_RANK, MLA_NOPE, MLA_ROPE, MLA_V = 8, 256, 128, 64, 32, 64
ROPE_THETA = 10000.0
DSA_HEADS, DSA_DH, IDX_HEADS, IDX_DIM, TOPK_MAX = 8, 64, 8, 64, 256
S5_GROUPS, S5_CH, S5_STATE = 32, 16, 64
S5_W = S5_GROUPS * S5_CH
S5_TC = 8
S5_GB = 8
D_FF = 4 * D_MODEL
LN_EPS = 1e-5
ALPHA = (2 * DEPTH) ** 0.25

LANES = 128
SUBLANES = 8
VMEM_LIMIT = 56 * 1024 * 1024
MASKED = -1e30
INT_MIN = -(2 ** 31)
NEG_INF_KEY = INT_MIN + 0x7FFFFF
LOG2E = math.log2(math.e)
SCORES_AHEAD = 3
COUNT_ROWS = 32
DV_ONES = 16
MLA_QSCALE = (MLA_NOPE + MLA_ROPE) ** -0.5 * LOG2E
DSA_QSCALE = DSA_DH ** -0.5 * LOG2E

L0_Q, L0_K, L0_V, L0_R, L0_CQ, L0_CKV, L0_MISC, L0_ROPE_A, L0_ROPE_B, L0_N = (
    0, 256, 512, 1024, 1536, 1792, 1920, 2048, 2176, 2304)
L1_Q, L1_QI, L1_K, L1_KI, L1_V, L1_NB = 0, 512, 1024, 1536, 1664, 2176
L1_U, L1_WI, L1_NF = 0, 512, 640


def _cparams(sem):
    return pltpu.CompilerParams(dimension_semantics=sem, vmem_limit_bytes=VMEM_LIMIT)


def _dot(a, b):
    return jnp.dot(a, b, preferred_element_type=F32)


def _dot_nt(a, b):
    return lax.dot_general(a, b, (((1,), (1,)), ((), ())), preferred_element_type=F32)


def _layernorm(v, g, b):
    mu = jnp.mean(v, -1, keepdims=True)
    d = v - mu
    var = jnp.mean(d * d, -1, keepdims=True)
    return d * lax.rsqrt(var + LN_EPS) * g + b


def _rmsnorm(v, g):
    return v * lax.rsqrt(jnp.mean(v * v, -1, keepdims=True) + LN_EPS) * g


def _whole(a, nargs):
    return pl.BlockSpec(a.shape, lambda *_: (0,) * a.ndim)


def _proj_kernel(x_ref, w_ref, o_ref, *, chunk):
    xb = x_ref[...].astype(BF16)
    for j in range(0, o_ref.shape[1], chunk):
        o_ref[:, j:j + chunk] = _dot(xb, w_ref[:, j:j + chunk]).astype(o_ref.dtype)


def _proj(x2, w, out_dtype, tm=512):
    t, k = x2.shape
    n = w.shape[1]
    return pl.pallas_call(
        functools.partial(_proj_kernel, chunk=256),
        grid=(t // tm,),
        in_specs=[pl.BlockSpec((tm, k), lambda i: (i, 0)),
                  pl.BlockSpec((k, n), lambda i: (0, 0))],
        out_specs=pl.BlockSpec((tm, n), lambda i: (i, 0)),
        out_shape=jax.ShapeDtypeStruct((t, n), out_dtype),
        compiler_params=_cparams(("parallel",)),
    )(x2, w)


def _gla_kernel(q_ref, k_ref, v_ref, r_ref, misc_ref, wg2_ref, bg_ref, gn_ref, o_ref, st_ref):
    c = GLA_CHUNK
    rows = q_ref.shape[0]

    @pl.when(pl.program_id(1) == 0)
    def _():
        st_ref[...] = jnp.zeros_like(st_ref)

    z = _dot(misc_ref[:, :GLA_RANK].astype(BF16), wg2_ref[...]) + bg_ref[...]
    log_a = jax.nn.log_sigmoid(z) * (1.0 / GLA_TAU)
    row = lax.broadcasted_iota(jnp.int32, (rows, rows), 0)
    col = lax.broadcasted_iota(jnp.int32, (rows, rows), 1)
    tril = jnp.where((col <= row) & (col >= row - row % c), 1.0, 0.0).astype(BF16)
    a_hi = log_a.astype(BF16)
    rem = log_a - a_hi.astype(F32)
    a_mid = rem.astype(BF16)
    a_lo = (rem - a_mid.astype(F32)).astype(BF16)
    cum = _dot(tril, a_hi) + _dot(tril, a_mid) + _dot(tril, a_lo)
    k = k_ref[...]
    q_dec = (q_ref[...] * (GLA_DK ** -0.5) * jnp.exp(cum)).astype(BF16)
    k_inv = (k * jnp.exp(-cum)).astype(BF16)
    causal = (lax.broadcasted_iota(jnp.int32, (c, c), 1) <= lax.broadcasted_iota(jnp.int32, (c, c), 0))
    for j in range(rows // c):
        rs = slice(j * c, (j + 1) * c)
        cum_last = cum[(j + 1) * c - 1:(j + 1) * c, :]
        k_end = (k[rs, :] * jnp.exp(cum_last - cum[rs, :])).astype(BF16)
        dec = jnp.exp(cum_last)
        for h in range(GLA_HEADS):
            ks = slice(h * GLA_DK, (h + 1) * GLA_DK)
            vs = slice(h * GLA_DV, (h + 1) * GLA_DV)
            v = v_ref[rs, vs]
            att = jnp.where(causal, _dot_nt(q_dec[rs, ks], k_inv[rs, ks]), 0.0)
            st = st_ref[h]
            o = _dot(att.astype(BF16), v.astype(BF16)) + _dot_nt(q_dec[rs, ks], st.astype(BF16))
            st_ref[h] = st * dec[:, ks] + _dot(v.T.astype(BF16), k_end[:, ks])
            o = _rmsnorm(o, gn_ref[:, vs])
            r = r_ref[rs, vs]
            o_ref[rs, vs] = (o * (r * jax.nn.sigmoid(r))).astype(o_ref.dtype)


def _gla(h0, wg2, bg, gn, b, l, chunks_per_step=4):
    c = GLA_CHUNK * min(chunks_per_step, l // GLA_CHUNK)
    nc = l // c
    hk = GLA_HEADS * GLA_DK
    hv = GLA_HEADS * GLA_DV

    def rows(j):
        return lambda bi, ci: (bi * nc + ci, j)

    return pl.pallas_call(
        _gla_kernel,
        grid=(b, nc),
        in_specs=[pl.BlockSpec((c, hk), rows(L0_Q // hk)),
                  pl.BlockSpec((c, hk), rows(L0_K // hk)),
                  pl.BlockSpec((c, hv), rows(L0_V // hv)),
                  pl.BlockSpec((c, hv), rows(L0_R // hv)),
                  pl.BlockSpec((c, LANES), rows(L0_MISC // LANES)),
                  pl.BlockSpec((GLA_RANK, hk), lambda bi, ci: (0, 0)),
                  pl.BlockSpec((1, hk), lambda bi, ci: (0, 0)),
                  pl.BlockSpec((1, hv), lambda bi, ci: (0, 0))],
        out_specs=pl.BlockSpec((c, hv), rows(0)),
        out_shape=jax.ShapeDtypeStruct((b * l, hv), BF16),
        scratch_shapes=[pltpu.VMEM((GLA_HEADS, GLA_DV, GLA_DK), F32)],
        compiler_params=_cparams(("parallel", "arbitrary")),
    )(h0, h0, h0, h0, h0, wg2, bg, gn)


def _mla_proj_kernel(cq_ref, ckv_ref, ra_ref, rb_ref, pos_ref, invf_ref, qn_ref, kvn_ref,
                     wqa_ref, wqb_ref, wk_ref, wv_ref, qt_ref, k_ref, vt_ref):
    tm = cq_ref.shape[0]
    pos = pos_ref[0].astype(F32)
    ang = invf_ref[...] * pos
    ones = jnp.ones((MLA_NOPE, tm), F32)
    pad = jnp.zeros((LANES - MLA_NOPE - MLA_ROPE, tm), F32)
    cos = jnp.concatenate([ones, jnp.cos(ang), pad], axis=0).T
    sin = jnp.concatenate([jnp.zeros((MLA_NOPE, tm), F32), jnp.sin(ang), pad], axis=0).T
    cqn = _rmsnorm(cq_ref[...], qn_ref[...]).astype(BF16)
    ckvn = _rmsnorm(ckv_ref[...], kvn_ref[...]).astype(BF16)
    k_rot = ra_ref[...] * cos + rb_ref[...] * sin
    for h in range(MLA_HEADS):
        hs = slice(h * LANES, (h + 1) * LANES)
        qh = _dot(cqn, wqa_ref[:, hs]) * cos + _dot(cqn, wqb_ref[:, hs]) * sin
        qt_ref[0, h] = (qh * MLA_QSCALE).T.astype(qt_ref.dtype)
        k_ref[0, h] = (_dot(ckvn, wk_ref[:, hs]) + k_rot).astype(k_ref.dtype)
    vt_ref[0] = _values_with_ones(_dot(ckvn, wv_ref[...]), MLA_HEADS, MLA_V).astype(vt_ref.dtype)


def _mla_proj(h0, pos3, invf, qn, kvn, wqa, wqb, wk, wv, b, l, tm):
    nt = l // tm

    def rows(j):
        return lambda bi, ti: (bi * nt + ti, j)

    consts = (invf, qn, kvn, wqa, wqb, wk, wv)
    return pl.pallas_call(
        _mla_proj_kernel,
        grid=(b, nt),
        in_specs=[pl.BlockSpec((tm, MLA_Q_RANK), rows(L0_CQ // MLA_Q_RANK)),
                  pl.BlockSpec((tm, MLA_KV_RANK), rows(L0_CKV // MLA_KV_RANK)),
                  pl.BlockSpec((tm, LANES), rows(L0_ROPE_A // LANES)),
                  pl.BlockSpec((tm, LANES), rows(L0_ROPE_B // LANES)),
                  pl.BlockSpec((1, 1, tm), lambda bi, ti: (bi * nt + ti, 0, 0))]
                 + [_whole(a, 2) for a in consts],
        out_specs=[pl.BlockSpec((1, MLA_HEADS, LANES, tm), lambda bi, ti: (bi, 0, 0, ti)),
                   pl.BlockSpec((1, MLA_HEADS, tm, LANES), lambda bi, ti: (bi, 0, ti, 0)),
                   pl.BlockSpec((1, MLA_HEADS * (MLA_V + DV_ONES), tm), lambda bi, ti: (bi, 0, ti))],
        out_shape=[jax.ShapeDtypeStruct((b, MLA_HEADS, LANES, l), BF16),
                   jax.ShapeDtypeStruct((b, MLA_HEADS, l, LANES), BF16),
                   jax.ShapeDtypeStruct((b, MLA_HEADS * (MLA_V + DV_ONES), l), BF16)],
        compiler_params=_cparams(("parallel", "parallel")),
    )(h0, h0, h0, h0, pos3, *consts)


def _flash_init(m_ref, acc_ref):
    m_ref[...] = jnp.full_like(m_ref, MASKED)
    acc_ref[...] = jnp.zeros_like(acc_ref)


def _flash_update(h, s, vt, m_ref, acc_ref):
    dve = vt.shape[0]
    hr = slice(h, h + 1)
    rows = slice(h * dve, (h + 1) * dve)
    m_old = m_ref[hr, :]
    m_new = jnp.maximum(m_old, jnp.max(s, axis=0, keepdims=True))
    p = jnp.exp2(s - m_new)
    acc_ref[rows, :] = jnp.exp2(m_old - m_new) * acc_ref[rows, :] + _dot(vt, p.astype(BF16))
    m_ref[hr, :] = m_new


def _flash_finish(o_ref, acc_ref, heads, dv):
    dve = dv + DV_ONES
    outs = [acc_ref[h * dve:h * dve + dv, :] / acc_ref[h * dve + dv:h * dve + dv + 1, :] for h in range(heads)]
    o_ref[...] = jnp.concatenate(outs, axis=0).T.astype(o_ref.dtype)


def _values_with_ones(v, heads, dv):
    vt = v.T
    ones = jnp.ones((DV_ONES, v.shape[0]), F32)
    return jnp.concatenate([piece for h in range(heads) for piece in (vt[h * dv:(h + 1) * dv, :], ones)], axis=0)


def _mla_attn_kernel(qt_ref, k_ref, vt_ref, o_ref, m_ref, acc_ref, *, tk):
    tq = qt_ref.shape[3]
    q0 = pl.program_id(1) * tq
    nkb = (q0 + tq + tk - 1) // tk
    nfull = (q0 + 1) // tk
    key_i = lax.broadcasted_iota(jnp.int32, (tk, tq), 0)
    qry_i = q0 + lax.broadcasted_iota(jnp.int32, (tk, tq), 1)
    dve = MLA_V + DV_ONES
    _flash_init(m_ref, acc_ref)

    def block(kb, carry, masked):
        ks = pl.ds(pl.multiple_of(kb * tk, tk), tk)
        def scores(h):
            return _dot(k_ref[0, h, ks, :], qt_ref[0, h])

        ahead = [scores(h) for h in range(SCORES_AHEAD)]
        for h in range(MLA_HEADS):
            s = ahead.pop(0)
            if h + SCORES_AHEAD < MLA_HEADS:
                ahead.append(scores(h + SCORES_AHEAD))
            if masked:
                s = jnp.where(kb * tk + key_i <= qry_i, s, MASKED)
            _flash_update(h, s, vt_ref[0, h * dve:(h + 1) * dve, ks], m_ref, acc_ref)
        return carry

    lax.fori_loop(0, nfull, functools.partial(block, masked=False), 0)
    lax.fori_loop(nfull, nkb, functools.partial(block, masked=True), 0)
    _flash_finish(o_ref, acc_ref, MLA_HEADS, MLA_V)


def _mla_attn(qt, k, vt, b, l, tq, tk):
    nq = l // tq
    hv = MLA_HEADS * MLA_V
    hve = MLA_HEADS * (MLA_V + DV_ONES)
    return pl.pallas_call(
        functools.partial(_mla_attn_kernel, tk=tk),
        grid=(b, nq),
        in_specs=[pl.BlockSpec((1, MLA_HEADS, LANES, tq), lambda bi, i: (bi, 0, 0, i)),
                  pl.BlockSpec((1, MLA_HEADS, l, LANES), lambda bi, i: (bi, 0, 0, 0), pipeline_mode=pl.Buffered(1)),
                  pl.BlockSpec((1, hve, l), lambda bi, i: (bi, 0, 0), pipeline_mode=pl.Buffered(1))],
        out_specs=pl.BlockSpec((tq, hv), lambda bi, i: (bi * nq + i, 0)),
        out_shape=jax.ShapeDtypeStruct((b * l, hv), BF16),
        scratch_shapes=[pltpu.VMEM((MLA_HEADS, tq), F32), pltpu.VMEM((hve, tq), F32)],
        compiler_params=_cparams(("parallel", "arbitrary")),
    )(qt, k, vt)


def _out_ln_kernel(a_ref, b_ref, x_ref, w_ref, g_ref, bias_ref, o_ref):
    na = a_ref.shape[1]
    mix = _dot(a_ref[...], w_ref[:na, :]) + _dot(b_ref[...], w_ref[na:, :])
    o_ref[...] = _layernorm(ALPHA * x_ref[...] + mix, g_ref[...], bias_ref[...])


def _out_ln(a, bb, x2, w, g, bias, tm=512):
    t = x2.shape[0]
    na, nb = a.shape[1], bb.shape[1]
    return pl.pallas_call(
        _out_ln_kernel,
        grid=(t // tm,),
        in_specs=[pl.BlockSpec((tm, na), lambda i: (i, 0)),
                  pl.BlockSpec((tm, nb), lambda i: (i, 0)),
                  pl.BlockSpec((tm, D_MODEL), lambda i: (i, 0)),
                  pl.BlockSpec((na + nb, D_MODEL), lambda i: (0, 0)),
                  pl.BlockSpec((1, D_MODEL), lambda i: (0, 0)),
                  pl.BlockSpec((1, D_MODEL), lambda i: (0, 0))],
        out_specs=pl.BlockSpec((tm, D_MODEL), lambda i: (i, 0)),
        out_shape=jax.ShapeDtypeStruct((t, D_MODEL), F32),
        compiler_params=_cparams(("parallel",)),
    )(a, bb, x2, w, g, bias)


def _mlp_ln_kernel(x_ref, w1_ref, w2_ref, g_ref, bias_ref, o_ref, acc_ref):
    j = pl.program_id(1)

    @pl.when(j == 0)
    def _():
        acc_ref[...] = jnp.zeros_like(acc_ref)

    hid = jnp.maximum(_dot(x_ref[...].astype(BF16), w1_ref[...]), 0.0)
    acc_ref[...] += _dot((hid * hid).astype(BF16), w2_ref[...])

    @pl.when(j == pl.num_programs(1) - 1)
    def _():
        o_ref[...] = _layernorm(ALPHA * x_ref[...] + acc_ref[...], g_ref[...], bias_ref[...])


def _mlp_ln(x2, w1, w2, g, bias, tm=1024, tf=1024):
    t = x2.shape[0]
    return pl.pallas_call(
        _mlp_ln_kernel,
        grid=(t // tm, D_FF // tf),
        in_specs=[pl.BlockSpec((tm, D_MODEL), lambda i, j: (i, 0)),
                  pl.BlockSpec((D_MODEL, tf), lambda i, j: (0, j)),
                  pl.BlockSpec((tf, D_MODEL), lambda i, j: (j, 0)),
                  pl.BlockSpec((1, D_MODEL), lambda i, j: (0, 0)),
                  pl.BlockSpec((1, D_MODEL), lambda i, j: (0, 0))],
        out_specs=pl.BlockSpec((tm, D_MODEL), lambda i, j: (i, 0)),
        out_shape=jax.ShapeDtypeStruct((t, D_MODEL), F32),
        scratch_shapes=[pltpu.VMEM((tm, D_MODEL), F32)],
        compiler_params=_cparams(("parallel", "arbitrary")),
    )(x2, w1, w2, g, bias)


def _proj1_kernel(x_ref, wb_ref, wf_ref, q_ref, qi_ref, k_ref, ki_ref, vt_ref, u_ref, wi_ref):
    xb = x_ref[...].astype(BF16)

    def cols(w_ref, start, width):
        return jnp.concatenate(
            [_dot(xb, w_ref[:, j:j + 256]) for j in range(start, start + width, 256)], axis=1)

    hd = DSA_HEADS * DSA_DH
    q_ref[...] = (cols(wb_ref, L1_Q, hd) * DSA_QSCALE).astype(BF16)
    qi_ref[...] = cols(wb_ref, L1_QI, hd).astype(BF16)
    k_ref[0] = cols(wb_ref, L1_K, hd).astype(BF16)
    ki_ref[0] = _dot(xb, wb_ref[:, L1_KI:L1_KI + LANES]).astype(BF16)
    vt_ref[0] = _values_with_ones(cols(wb_ref, L1_V, hd), DSA_HEADS, DSA_DH).astype(BF16)
    u_ref[...] = cols(wf_ref, L1_U, S5_W)
    wi_ref[...] = _dot(xb, wf_ref[:, L1_WI:L1_WI + LANES])


def _proj1(x2, wb, wf, b, l, tm=512):
    tm = min(tm, l)
    nt = l // tm
    t = b * l
    hd = DSA_HEADS * DSA_DH
    hde = DSA_HEADS * (DSA_DH + DV_ONES)
    flat = lambda bi, ti: (bi * nt + ti, 0)
    return pl.pallas_call(
        _proj1_kernel,
        grid=(b, nt),
        in_specs=[pl.BlockSpec((tm, D_MODEL), flat), _whole(wb, 2), _whole(wf, 2)],
        out_specs=[pl.BlockSpec((tm, hd), flat),
                   pl.BlockSpec((tm, hd), flat),
                   pl.BlockSpec((1, tm, hd), lambda bi, ti: (bi, ti, 0)),
                   pl.BlockSpec((1, tm, LANES), lambda bi, ti: (bi, ti, 0)),
                   pl.BlockSpec((1, hde, tm), lambda bi, ti: (bi, 0, ti)),
                   pl.BlockSpec((tm, S5_W), flat),
                   pl.BlockSpec((tm, LANES), flat)],
        out_shape=[jax.ShapeDtypeStruct((t, hd), BF16),
                   jax.ShapeDtypeStruct((t, hd), BF16),
                   jax.ShapeDtypeStruct((b, l, hd), BF16),
                   jax.ShapeDtypeStruct((b, l, LANES), BF16),
                   jax.ShapeDtypeStruct((b, hde, l), BF16),
                   jax.ShapeDtypeStruct((t, S5_W), F32),
                   jax.ShapeDtypeStruct((t, LANES), F32)],
        compiler_params=_cparams(("parallel", "parallel")),
    )(x2, wb, wf)


def _dsa_kernel(q_ref, qi_ref, wi_ref, k_ref, vt_ref, ki_ref, o_ref, key_ref, m_ref, acc_ref,
                *, topk, tk, seq):
    tq = q_ref.shape[0]
    q0 = pl.program_id(1) * tq
    nkb = (q0 + tq + tk - 1) // tk
    tc = math.gcd(tk, tq)
    nkc = (q0 + tq + tc - 1) // tc
    key_i = lax.broadcasted_iota(jnp.int32, (tk, tq), 0)
    qry_i = q0 + lax.broadcasted_iota(jnp.int32, (tk, tq), 1)
    qry_row = q0 + lax.broadcasted_iota(jnp.int32, (1, tq), 1)

    def kslice(kb):
        return pl.ds(pl.multiple_of(kb * tk, tk), tk)

    def head_rows(xt, h, width):
        per = LANES // width
        pair = xt[(h // per) * LANES:(h // per + 1) * LANES, :]
        r = lax.broadcasted_iota(jnp.int32, pair.shape, 0)
        lo = (h % per) * width
        return jnp.where((r >= lo) & (r < lo + width), pair, 0.0).astype(BF16)

    w_t = wi_ref[...].T[:IDX_HEADS, :] * (IDX_HEADS ** -0.5) * (IDX_DIM ** -0.5)
    qi_t = qi_ref[...].astype(F32).T
    zero_rows = jnp.zeros((LANES - IDX_DIM, tq), BF16)
    qi_heads = [jnp.concatenate([qi_t[h * IDX_DIM:(h + 1) * IDX_DIM, :].astype(BF16), zero_rows], axis=0)
                for h in range(IDX_HEADS)]

    def score_block(kb, carry):
        kib = ki_ref[0, kslice(kb), :]
        sc = jnp.zeros((tk, tq), F32)
        for h in range(IDX_HEADS):
            sc = sc + jnp.maximum(_dot(kib, qi_heads[h]), 0.0) * w_t[h:h + 1, :]
        sc = jnp.where(kb * tk + key_i <= qry_i, sc, -jnp.inf)
        bits = pltpu.bitcast(sc, jnp.int32)
        key_ref[kslice(kb), :] = jnp.where(bits < 0, bits ^ jnp.int32(0x7FFFFFFF), bits)
        return carry

    lax.fori_loop(0, nkb, score_block, 0)

    def count(pred):
        def body(kc, acc):
            keys = key_ref[pl.ds(pl.multiple_of(kc * tc, tc), tc), :]
            hit = jnp.where(pred(keys, kc * tc), 1, 0)
            return acc + jnp.sum(hit.reshape(tc // COUNT_ROWS, COUNT_ROWS, tq), axis=0)
        acc = lax.fori_loop(0, nkc, body, jnp.zeros((COUNT_ROWS, tq), jnp.int32))
        return jnp.sum(acc, axis=0, keepdims=True)

    n_nonneg = count(lambda ks, k0: ks >= 0)
    nonneg = n_nonneg >= topk
    start = (jnp.where(nonneg, 0, INT_MIN).astype(jnp.int32), jnp.where(nonneg, n_nonneg, nkc * tc))

    def bit_step(b, carry):
        prefix, n_prefix = carry
        cand = prefix | jnp.left_shift(jnp.int32(1), 30 - b)
        n_ge = count(lambda ks, k0: ks >= cand)
        take = n_ge >= topk
        return jnp.where(take, cand, prefix), jnp.where(take, n_ge, n_prefix)

    thr, n_ge_thr = lax.fori_loop(0, 31, bit_step, start)
    finite = thr != NEG_INF_KEY

    def tie_cut():
        need = topk - count(lambda ks, k0: ks > thr)
        nbits = max(1, (seq - 1).bit_length())
        key_c = lax.broadcasted_iota(jnp.int32, (tc, tq), 0)

        def step(b, ans):
            cand = ans | jnp.left_shift(jnp.int32(1), nbits - 1 - b)
            n_before = count(lambda ks, k0: (ks == thr) & (k0 + key_c < cand))
            return jnp.where(n_before < need, cand, ans)
        return lax.fori_loop(0, nbits, step, jnp.zeros((1, tq), jnp.int32))

    has_tie = jnp.max(jnp.where(finite & (n_ge_thr > topk), 1, 0)) > 0
    cut = lax.cond(has_tie, tie_cut, lambda: jnp.full((1, tq), seq, jnp.int32))
    cut = jnp.where(finite, cut, qry_row)

    def bias_block(kb, carry):
        keys = key_ref[kslice(kb), :]
        sel = (keys > thr) | ((keys == thr) & (kb * tk + key_i <= cut))
        key_ref[kslice(kb), :] = pltpu.bitcast(jnp.where(sel, jnp.inf, MASKED), jnp.int32)
        return carry

    lax.fori_loop(0, nkb, bias_block, 0)

    q_t = q_ref[...].astype(F32).T
    per = LANES // DSA_DH
    dve = DSA_DH + DV_ONES
    q_heads = [head_rows(q_t, h, DSA_DH) for h in range(DSA_HEADS)]
    _flash_init(m_ref, acc_ref)

    def attend(kb, carry):
        ks = kslice(kb)
        cap = pltpu.bitcast(key_ref[ks, :], F32)

        def scores(h):
            return _dot(k_ref[0, ks, (h // per) * LANES:(h // per + 1) * LANES], q_heads[h])

        ahead = [scores(h) for h in range(SCORES_AHEAD)]
        for h in range(DSA_HEADS):
            s = jnp.minimum(ahead.pop(0), cap)
            if h + SCORES_AHEAD < DSA_HEADS:
                ahead.append(scores(h + SCORES_AHEAD))
            _flash_update(h, s, vt_ref[0, h * dve:(h + 1) * dve, ks], m_ref, acc_ref)
        return carry

    lax.fori_loop(0, nkb, attend, 0)
    _flash_finish(o_ref, acc_ref, DSA_HEADS, DSA_DH)


def _dsa(q, qi, wi, k, vt, ki, b, l, tq, tk):
    nq = l // tq
    topk = min(TOPK_MAX, l // 4)
    hd = DSA_HEADS * DSA_DH
    hde = DSA_HEADS * (DSA_DH + DV_ONES)
    flat = lambda bi, i: (bi * nq + i, 0)
    batch = lambda bi, i: (bi, 0, 0)
    return pl.pallas_call(
        functools.partial(_dsa_kernel, topk=topk, tk=tk, seq=l),
        grid=(b, nq),
        in_specs=[pl.BlockSpec((tq, hd), flat),
                  pl.BlockSpec((tq, hd), flat),
                  pl.BlockSpec((tq, LANES), flat),
                  pl.BlockSpec((1, l, hd), batch, pipeline_mode=pl.Buffered(1)),
                  pl.BlockSpec((1, hde, l), batch, pipeline_mode=pl.Buffered(1)),
                  pl.BlockSpec((1, l, LANES), batch, pipeline_mode=pl.Buffered(1))],
        out_specs=pl.BlockSpec((tq, hd), flat),
        out_shape=jax.ShapeDtypeStruct((b * l, hd), BF16),
        scratch_shapes=[pltpu.VMEM((l, tq), jnp.int32), pltpu.VMEM((DSA_HEADS, tq), F32),
                        pltpu.VMEM((hde, tq), F32)],
        compiler_params=_cparams(("parallel", "arbitrary")),
    )(q, qi, wi, k, vt, ki)


def _s5_kernel(u_ref, m1_ref, m2_ref, m3_ref, pa_ref, pb_ref, y_ref):
    tc = S5_TC
    rows = u_ref.shape[0] // tc
    u = jnp.concatenate([u_ref[pl.ds(s, rows, stride=tc), :] for s in range(tc)], axis=1).astype(BF16)
    x = _dot(u, m2_ref[0])
    pos = lax.broadcasted_iota(jnp.int32, x.shape, 0)
    half = x.shape[1] // 2
    d, k = 1, 0
    while d < rows:
        sh = jnp.where(pos >= d, pltpu.roll(x, d, axis=0), 0.0)
        x = x + sh * pa_ref[0, k:k + 1, :] + pltpu.roll(sh, half, axis=1) * pb_ref[0, k:k + 1, :]
        d, k = 2 * d, k + 1
    x_in = jnp.where(pos >= 1, pltpu.roll(x, 1, axis=0), 0.0)
    y = _dot(u, m1_ref[0]) + _dot(x_in.astype(BF16), m3_ref[0])
    for t in range(tc):
        y_ref[pl.ds(t, rows, stride=tc), :] = y[:, t * LANES:(t + 1) * LANES]


def _s5(u, m1, m2, m3, pa, pb, b, l):
    nblk = S5_W // LANES
    per_block = lambda a: pl.BlockSpec((1,) + a.shape[1:], lambda ci, bi: (ci,) + (0,) * (a.ndim - 1))
    return pl.pallas_call(
        _s5_kernel,
        grid=(nblk, b),
        in_specs=[pl.BlockSpec((l, LANES), lambda ci, bi: (bi, ci))] + [per_block(a) for a in (m1, m2, m3, pa, pb)],
        out_specs=pl.BlockSpec((l, LANES), lambda ci, bi: (bi, ci)),
        out_shape=jax.ShapeDtypeStruct((b * l, S5_W), F32),
        compiler_params=_cparams(("parallel", "parallel")),
    )(u, m1, m2, m3, pa, pb)


def _s5_out_kernel(y_ref, u_ref, d_ref, gw_ref, gb_ref, o_ref):
    y = jax.nn.gelu(y_ref[...] + d_ref[...] * u_ref[...])
    gate = jax.nn.sigmoid(_dot(y.astype(BF16), gw_ref[...]) + gb_ref[...])
    o_ref[...] = (y * gate).astype(o_ref.dtype)


def _s5_out(y, u, d, gw, gb, tm=1024):
    t = y.shape[0]
    tm = min(tm, t)
    return pl.pallas_call(
        _s5_out_kernel,
        grid=(t // tm,),
        in_specs=[pl.BlockSpec((tm, S5_W), lambda i: (i, 0)),
                  pl.BlockSpec((tm, S5_W), lambda i: (i, 0)),
                  _whole(d, 1), _whole(gw, 1), _whole(gb, 1)],
        out_specs=pl.BlockSpec((tm, S5_W), lambda i: (i, 0)),
        out_shape=jax.ShapeDtypeStruct((t, S5_W), BF16),
        compiler_params=_cparams(("parallel",)),
    )(y, u, d, gw, gb)


def _pad_cols(w, n):
    return jnp.pad(w, ((0, 0), (0, n - w.shape[1])))


def _l0_weights(w_in, w_uq, w_ukv):
    q, k, v, g_lr, r, c_q, c_kv, k_rope = jnp.split(
        w_in, [256, 512, 1024, 1040, 1552, 1808, 1936], axis=1)
    half = MLA_ROPE // 2
    t1, t2 = k_rope[:, :half], k_rope[:, half:]
    z = lambda n: jnp.zeros((w_in.shape[0], n), w_in.dtype)
    tail = LANES - MLA_NOPE - MLA_ROPE
    rope_a = jnp.concatenate([z(MLA_NOPE), t1, t2, z(tail)], axis=1)
    rope_b = jnp.concatenate([z(MLA_NOPE), -t2, t1, z(tail)], axis=1)
    w0 = jnp.concatenate([q, k, v, r, c_q, c_kv, _pad_cols(g_lr, LANES), rope_a, rope_b], axis=1)
    uq = w_uq.reshape(MLA_Q_RANK, MLA_HEADS, MLA_NOPE + MLA_ROPE)
    nope, r1, r2 = uq[..., :MLA_NOPE], uq[..., MLA_NOPE:MLA_NOPE + half], uq[..., MLA_NOPE + half:]
    zq = lambda n: jnp.zeros((MLA_Q_RANK, MLA_HEADS, n), w_uq.dtype)
    wqa = jnp.concatenate([nope, r1, r2, zq(tail)], axis=-1).reshape(MLA_Q_RANK, MLA_HEADS * LANES)
    wqb = jnp.concatenate([zq(MLA_NOPE), -r2, r1, zq(tail)], axis=-1).reshape(MLA_Q_RANK, MLA_HEADS * LANES)
    ukv = w_ukv.reshape(MLA_KV_RANK, MLA_HEADS, MLA_NOPE + MLA_V)
    wk = jnp.pad(ukv[..., :MLA_NOPE], ((0, 0), (0, 0), (0, LANES - MLA_NOPE))).reshape(MLA_KV_RANK, MLA_HEADS * LANES)
    wv = ukv[..., MLA_NOPE:].reshape(MLA_KV_RANK, MLA_HEADS * MLA_V)
    return w0.astype(BF16), wqa.astype(BF16), wqb.astype(BF16), wk.astype(BF16), wv.astype(BF16)


def _l1_weights(w_in):
    q, k, v, qi, ki, wi, u = jnp.split(w_in, [512, 1024, 1536, 2048, 2112, 2120], axis=1)
    wb = jnp.concatenate([q, qi, k, _pad_cols(ki, LANES), v], axis=1)
    wf = jnp.concatenate([u, _pad_cols(wi, LANES)], axis=1)
    return wb.astype(BF16), wf.astype(BF16)


def _s5_weights(a_re, a_im, b_re, b_im, c_re, c_im, log_step, nsteps):
    lam_re = jnp.minimum(a_re, -1e-4)
    lam_im = a_im
    dt = jnp.exp(log_step)[:, None]
    mag = jnp.exp(lam_re * dt)
    abar_re = mag * jnp.cos(lam_im * dt)
    abar_im = mag * jnp.sin(lam_im * dt)
    den = jnp.square(lam_re) + jnp.square(lam_im)
    nr = abar_re - 1.0
    ni = abar_im
    coef_re = (nr * lam_re + ni * lam_im) / den
    coef_im = (ni * lam_re - nr * lam_im) / den
    bbar_re = coef_re[..., None] * b_re - coef_im[..., None] * b_im
    bbar_im = coef_re[..., None] * b_im + coef_im[..., None] * b_re
    tc, gb, p, hc = S5_TC, S5_GB, S5_STATE, S5_CH
    nblk = S5_GROUPS // gb
    pr, pi = [jnp.ones_like(abar_re)], [jnp.zeros_like(abar_im)]
    for _ in range(tc):
        pr, pi = (pr + [pr[-1] * abar_re - pi[-1] * abar_im], pi + [pr[-1] * abar_im + pi[-1] * abar_re])
    pr, pi = jnp.stack(pr), jnp.stack(pi)
    ab_re = pr[:tc, :, :, None] * bbar_re - pi[:tc, :, :, None] * bbar_im
    ab_im = pr[:tc, :, :, None] * bbar_im + pi[:tc, :, :, None] * bbar_re
    eye = jnp.eye(gb, dtype=F32)
    blocks = lambda a: a.reshape(a.shape[0], nblk, gb, *a.shape[2:])
    taps = jnp.einsum('gop,dgpi->dgoi', c_re, ab_re) - jnp.einsum('gop,dgpi->dgoi', c_im, ab_im)
    lag = jnp.arange(tc)[None, :] - jnp.arange(tc)[:, None]
    k_st = jnp.where((lag >= 0)[:, :, None, None, None], taps[jnp.clip(lag, 0, tc - 1)], 0.0)
    k_st = k_st.reshape(tc, tc, nblk, gb, hc, hc)
    m1 = jnp.einsum('stbgoi,gh->bsgitho', k_st, eye).reshape(nblk, tc * gb * hc, tc * gb * hc)
    ab = jnp.stack([blocks(ab_re[::-1]), blocks(ab_im[::-1])])
    m2 = jnp.einsum('rsbgpi,gh->bsgirhp', ab, eye).reshape(nblk, tc * gb * hc, 2 * gb * p)
    d_re = c_re[None] * pr[1:, :, None, :] - c_im[None] * pi[1:, :, None, :]
    d_im = c_re[None] * pi[1:, :, None, :] + c_im[None] * pr[1:, :, None, :]
    dd = jnp.stack([blocks(d_re), -blocks(d_im)])
    m3 = jnp.einsum('rtbgop,gh->brgptho', dd, eye).reshape(nblk, 2 * gb * p, tc * gb * hc)
    qr, qi, pa, pb = pr[tc].reshape(nblk, gb * p), pi[tc].reshape(nblk, gb * p), [], []
    for _ in range(nsteps):
        pa.append(jnp.concatenate([qr, qr], axis=-1))
        pb.append(jnp.concatenate([-qi, qi], axis=-1))
        qr, qi = qr * qr - qi * qi, 2.0 * qr * qi
    return m1.astype(BF16), m2.astype(BF16), m3.astype(BF16), jnp.stack(pa, axis=1), jnp.stack(pb, axis=1)


def kernel(x, positions, l0_w_in, l0_gla_wg2, l0_gla_bg, l0_gla_norm, l0_mla_q_norm, l0_mla_w_uq, l0_mla_kv_norm, l0_mla_w_ukv, l0_w_out, l0_ln1_g, l0_ln1_b, l0_mlp_w1, l0_mlp_w2, l0_ln2_g, l0_ln2_b, l1_w_in, l1_s5_a_re, l1_s5_a_im, l1_s5_b_re, l1_s5_b_im, l1_s5_c_re, l1_s5_c_im, l1_s5_d, l1_s5_log_step, l1_glu_w, l1_glu_b, l1_w_out, l1_ln1_g, l1_ln1_b, l1_mlp_w1, l1_mlp_w2, l1_ln2_g, l1_ln2_b):
    b, l, _ = x.shape
    t = b * l
    row = lambda a: a.reshape(1, -1)
    x2 = x.reshape(t, D_MODEL)
    tq = min(256, l)
    tk = min(512, l)

    w0, wqa, wqb, wk, wv = _l0_weights(l0_w_in, l0_mla_w_uq, l0_mla_w_ukv)
    h0 = _proj(x2, w0, F32)
    o_gla = _gla(h0, l0_gla_wg2.astype(BF16), row(l0_gla_bg), row(l0_gla_norm), b, l)
    tm = min(512, l)
    half = MLA_ROPE // 2
    inv_freq = ROPE_THETA ** (-jnp.arange(half, dtype=F32) / half)
    invf = jnp.concatenate([inv_freq, inv_freq]).reshape(MLA_ROPE, 1)
    pos3 = positions.reshape(t // tm, 1, tm)
    qt, km, vt = _mla_proj(h0, pos3, invf, row(l0_mla_q_norm), row(l0_mla_kv_norm), wqa, wqb, wk, wv, b, l, tm)
    o_mla = _mla_attn(qt, km, vt, b, l, tq, tk)
    x2 = _out_ln(o_gla, o_mla, x2, l0_w_out.astype(BF16), row(l0_ln1_g), row(l0_ln1_b))
    x2 = _mlp_ln(x2, l0_mlp_w1.astype(BF16), l0_mlp_w2.astype(BF16), row(l0_ln2_g), row(l0_ln2_b))

    w1b, w1f = _l1_weights(l1_w_in)
    q1, qi1, k1, ki1, vt1, u1, wi1 = _proj1(x2, w1b, w1f, b, l)
    o_dsa = _dsa(q1, qi1, wi1, k1, vt1, ki1, b, l, tq, tk)
    seg = l // S5_TC
    s5w = _s5_weights(l1_s5_a_re, l1_s5_a_im, l1_s5_b_re, l1_s5_b_im, l1_s5_c_re, l1_s5_c_im, l1_s5_log_step,
                      max(1, (seg - 1).bit_length()))
    y1 = _s5(u1, *s5w, b, l)
    o_s5 = _s5_out(y1, u1, row(l1_s5_d), l1_glu_w.astype(BF16), row(l1_glu_b))
    x2 = _out_ln(o_dsa, o_s5, x2, l1_w_out.astype(BF16), row(l1_ln1_g), row(l1_ln1_b))
    x2 = _mlp_ln(x2, l1_mlp_w1.astype(BF16), l1_mlp_w2.astype(BF16), row(l1_ln2_g), row(l1_ln2_b))
    return x2.reshape(b, l, D_MODEL)
```

```python
import functools
import math

import jax
import jax.numpy as jnp
from jax import lax
from jax.experimental import pallas as pl
from jax.experimental.pallas import tpu as pltpu

BF16 = jnp.bfloat16
F32 = jnp.float32

D_MODEL = 1024
DEPTH = 2
GLA_HEADS, GLA_DK, GLA_DV, GLA_RANK, GLA_TAU, GLA_CHUNK = 4, 64, 128, 16, 16.0, 64
MLA_HEADS, MLA_Q_RANK, MLA_KV_RANK, MLA_NOPE, MLA_ROPE, MLA_V = 8, 256, 128, 64, 32, 64
ROPE_THETA = 10000.0
DSA_HEADS, DSA_DH, IDX_HEADS, IDX_DIM, TOPK_MAX = 8, 64, 8, 64, 256
S5_GROUPS, S5_CH, S5_STATE = 32, 16, 64
S5_W = S5_GROUPS * S5_CH
S5_TC = 8
S5_GB = 8
D_FF = 4 * D_MODEL
LN_EPS = 1e-5
ALPHA = (2 * DEPTH) ** 0.25

LANES = 128
SUBLANES = 8
VMEM_LIMIT = 56 * 1024 * 1024
MASKED = -1e30
INT_MIN = -(2 ** 31)
NEG_INF_KEY = INT_MIN + 0x7FFFFF
LOG2E = math.log2(math.e)
SCORES_AHEAD = 3
BLOCKS_PER_TRIP = (4, 2, 1)
COUNT_ROWS = 32
DV_ONES = 16
MLA_QSCALE = (MLA_NOPE + MLA_ROPE) ** -0.5 * LOG2E
DSA_QSCALE = DSA_DH ** -0.5 * LOG2E

L0_Q, L0_K, L0_V, L0_R, L0_CQ, L0_CKV, L0_MISC, L0_ROPE_A, L0_ROPE_B, L0_N = (
    0, 256, 512, 1024, 1536, 1792, 1920, 2048, 2176, 2304)
L1_Q, L1_QI, L1_K, L1_KI, L1_V, L1_NB = 0, 512, 1024, 1536, 1664, 2176
L1_U, L1_WI, L1_NF = 0, 512, 640


def _cparams(sem):
    return pltpu.CompilerParams(dimension_semantics=sem, vmem_limit_bytes=VMEM_LIMIT)


def _dot(a, b):
    return jnp.dot(a, b, preferred_element_type=F32)


def _dot_nt(a, b):
    return lax.dot_general(a, b, (((1,), (1,)), ((), ())), preferred_element_type=F32)


def _layernorm(v, g, b):
    mu = jnp.mean(v, -1, keepdims=True)
    d = v - mu
    var = jnp.mean(d * d, -1, keepdims=True)
    return d * lax.rsqrt(var + LN_EPS) * g + b


def _rmsnorm(v, g):
    return v * lax.rsqrt(jnp.mean(v * v, -1, keepdims=True) + LN_EPS) * g


def _whole(a, nargs):
    return pl.BlockSpec(a.shape, lambda *_: (0,) * a.ndim)


def _proj_kernel(x_ref, w_ref, o_ref, *, chunk):
    xb = x_ref[...].astype(BF16)
    for j in range(0, o_ref.shape[1], chunk):
        o_ref[:, j:j + chunk] = _dot(xb, w_ref[:, j:j + chunk]).astype(o_ref.dtype)


def _proj(x2, w, out_dtype, tm=512):
    t, k = x2.shape
    n = w.shape[1]
    return pl.pallas_call(
        functools.partial(_proj_kernel, chunk=256),
        grid=(t // tm,),
        in_specs=[pl.BlockSpec((tm, k), lambda i: (i, 0)),
                  pl.BlockSpec((k, n), lambda i: (0, 0))],
        out_specs=pl.BlockSpec((tm, n), lambda i: (i, 0)),
        out_shape=jax.ShapeDtypeStruct((t, n), out_dtype),
        compiler_params=_cparams(("parallel",)),
    )(x2, w)


def _gla_kernel(q_ref, k_ref, v_ref, r_ref, misc_ref, wg2_ref, bg_ref, gn_ref, o_ref, st_ref):
    c = GLA_CHUNK
    rows = q_ref.shape[0]

    @pl.when(pl.program_id(1) == 0)
    def _():
        st_ref[...] = jnp.zeros_like(st_ref)

    z = _dot(misc_ref[:, :GLA_RANK].astype(BF16), wg2_ref[...]) + bg_ref[...]
    log_a = jax.nn.log_sigmoid(z) * (1.0 / GLA_TAU)
    row = lax.broadcasted_iota(jnp.int32, (rows, rows), 0)
    col = lax.broadcasted_iota(jnp.int32, (rows, rows), 1)
    tril = jnp.where((col <= row) & (col >= row - row % c), 1.0, 0.0).astype(BF16)
    a_hi = log_a.astype(BF16)
    rem = log_a - a_hi.astype(F32)
    a_mid = rem.astype(BF16)
    a_lo = (rem - a_mid.astype(F32)).astype(BF16)
    cum = _dot(tril, a_hi) + _dot(tril, a_mid) + _dot(tril, a_lo)
    k = k_ref[...]
    q_dec = (q_ref[...] * (GLA_DK ** -0.5) * jnp.exp(cum)).astype(BF16)
    k_inv = (k * jnp.exp(-cum)).astype(BF16)
    causal = (lax.broadcasted_iota(jnp.int32, (c, c), 1) <= lax.broadcasted_iota(jnp.int32, (c, c), 0))
    for j in range(rows // c):
        rs = slice(j * c, (j + 1) * c)
        cum_last = cum[(j + 1) * c - 1:(j + 1) * c, :]
        k_end = (k[rs, :] * jnp.exp(cum_last - cum[rs, :])).astype(BF16)
        dec = jnp.exp(cum_last)
        for h in range(GLA_HEADS):
            ks = slice(h * GLA_DK, (h + 1) * GLA_DK)
            vs = slice(h * GLA_DV, (h + 1) * GLA_DV)
            v = v_ref[rs, vs]
            att = jnp.where(causal, _dot_nt(q_dec[rs, ks], k_inv[rs, ks]), 0.0)
            st = st_ref[h]
            o = _dot(att.astype(BF16), v.astype(BF16)) + _dot_nt(q_dec[rs, ks], st.astype(BF16))
            st_ref[h] = st * dec[:, ks] + _dot(v.T.astype(BF16), k_end[:, ks])
            o = _rmsnorm(o, gn_ref[:, vs])
            r = r_ref[rs, vs]
            o_ref[rs, vs] = (o * (r * jax.nn.sigmoid(r))).astype(o_ref.dtype)


def _gla(h0, wg2, bg, gn, b, l, chunks_per_step=4):
    c = GLA_CHUNK * min(chunks_per_step, l // GLA_CHUNK)
    nc = l // c
    hk = GLA_HEADS * GLA_DK
    hv = GLA_HEADS * GLA_DV

    def rows(j):
        return lambda bi, ci: (bi * nc + ci, j)

    return pl.pallas_call(
        _gla_kernel,
        grid=(b, nc),
        in_specs=[pl.BlockSpec((c, hk), rows(L0_Q // hk)),
                  pl.BlockSpec((c, hk), rows(L0_K // hk)),
                  pl.BlockSpec((c, hv), rows(L0_V // hv)),
                  pl.BlockSpec((c, hv), rows(L0_R // hv)),
                  pl.BlockSpec((c, LANES), rows(L0_MISC // LANES)),
                  pl.BlockSpec((GLA_RANK, hk), lambda bi, ci: (0, 0)),
                  pl.BlockSpec((1, hk), lambda bi, ci: (0, 0)),
                  pl.BlockSpec((1, hv), lambda bi, ci: (0, 0))],
        out_specs=pl.BlockSpec((c, hv), rows(0)),
        out_shape=jax.ShapeDtypeStruct((b * l, hv), BF16),
        scratch_shapes=[pltpu.VMEM((GLA_HEADS, GLA_DV, GLA_DK), F32)],
        compiler_params=_cparams(("parallel", "arbitrary")),
    )(h0, h0, h0, h0, h0, wg2, bg, gn)


def _mla_proj_kernel(cq_ref, ckv_ref, ra_ref, rb_ref, pos_ref, invf_ref, qn_ref, kvn_ref,
                     wqa_ref, wqb_ref, wk_ref, wv_ref, qt_ref, k_ref, vt_ref):
    tm = cq_ref.shape[0]
    pos = pos_ref[0].astype(F32)
    ang = invf_ref[...] * pos
    ones = jnp.ones((MLA_NOPE, tm), F32)
    pad = jnp.zeros((LANES - MLA_NOPE - MLA_ROPE, tm), F32)
    cos = jnp.concatenate([ones, jnp.cos(ang), pad], axis=0).T
    sin = jnp.concatenate([jnp.zeros((MLA_NOPE, tm), F32), jnp.sin(ang), pad], axis=0).T
    cqn = _rmsnorm(cq_ref[...], qn_ref[...]).astype(BF16)
    ckvn = _rmsnorm(ckv_ref[...], kvn_ref[...]).astype(BF16)
    k_rot = ra_ref[...] * cos + rb_ref[...] * sin
    for h in range(MLA_HEADS):
        hs = slice(h * LANES, (h + 1) * LANES)
        qh = _dot(cqn, wqa_ref[:, hs]) * cos + _dot(cqn, wqb_ref[:, hs]) * sin
        qt_ref[0, h] = (qh * MLA_QSCALE).T.astype(qt_ref.dtype)
        k_ref[0, h] = (_dot(ckvn, wk_ref[:, hs]) + k_rot).astype(k_ref.dtype)
    vt_ref[0] = _values_with_ones(_dot(ckvn, wv_ref[...]), MLA_HEADS, MLA_V).astype(vt_ref.dtype)


def _mla_proj(h0, pos3, invf, qn, kvn, wqa, wqb, wk, wv, b, l, tm):
    nt = l // tm

    def rows(j):
        return lambda bi, ti: (bi * nt + ti, j)

    consts = (invf, qn, kvn, wqa, wqb, wk, wv)
    return pl.pallas_call(
        _mla_proj_kernel,
        grid=(b, nt),
        in_specs=[pl.BlockSpec((tm, MLA_Q_RANK), rows(L0_CQ // MLA_Q_RANK)),
                  pl.BlockSpec((tm, MLA_KV_RANK), rows(L0_CKV // MLA_KV_RANK)),
                  pl.BlockSpec((tm, LANES), rows(L0_ROPE_A // LANES)),
                  pl.BlockSpec((tm, LANES), rows(L0_ROPE_B // LANES)),
                  pl.BlockSpec((1, 1, tm), lambda bi, ti: (bi * nt + ti, 0, 0))]
                 + [_whole(a, 2) for a in consts],
        out_specs=[pl.BlockSpec((1, MLA_HEADS, LANES, tm), lambda bi, ti: (bi, 0, 0, ti)),
                   pl.BlockSpec((1, MLA_HEADS, tm, LANES), lambda bi, ti: (bi, 0, ti, 0)),
                   pl.BlockSpec((1, MLA_HEADS * (MLA_V + DV_ONES), tm), lambda bi, ti: (bi, 0, ti))],
        out_shape=[jax.ShapeDtypeStruct((b, MLA_HEADS, LANES, l), BF16),
                   jax.ShapeDtypeStruct((b, MLA_HEADS, l, LANES), BF16),
                   jax.ShapeDtypeStruct((b, MLA_HEADS * (MLA_V + DV_ONES), l), BF16)],
        compiler_params=_cparams(("parallel", "parallel")),
    )(h0, h0, h0, h0, pos3, *consts)


def _flash_init(m_ref, acc_ref):
    m_ref[...] = jnp.full_like(m_ref, MASKED)
    acc_ref[...] = jnp.zeros_like(acc_ref)


def _flash_update(h, s, vt, m_ref, acc_ref):
    dve = vt.shape[0]
    hr = slice(h, h + 1)
    rows = slice(h * dve, (h + 1) * dve)
    m_old = m_ref[hr, :]
    m_new = jnp.maximum(m_old, jnp.max(s, axis=0, keepdims=True))
    p = jnp.exp2(s - m_new)
    acc_ref[rows, :] = jnp.exp2(m_old - m_new) * acc_ref[rows, :] + _dot(vt, p.astype(BF16))
    m_ref[hr, :] = m_new


def _blocked_loop(lo, hi, body, carry):
    for size in BLOCKS_PER_TRIP:
        ntrips = (hi - lo) // size

        def trip(i, c, lo=lo, size=size):
            for j in range(size):
                c = body(lo + i * size + j, c)
            return c

        carry = lax.fori_loop(0, ntrips, trip, carry)
        lo = lo + ntrips * size
    return carry


def _key_block_loops(lo, hi, heads, scores, update):
    def trip(first, nblocks):
        steps = [(first + j, h) for j in range(nblocks) for h in range(heads)]
        ahead = [scores(*st) for st in steps[:SCORES_AHEAD]]
        for i, (kb, h) in enumerate(steps):
            s = ahead.pop(0)
            if i + SCORES_AHEAD < len(steps):
                ahead.append(scores(*steps[i + SCORES_AHEAD]))
            update(kb, h, s)

    for size in BLOCKS_PER_TRIP:
        ntrips = (hi - lo) // size

        def body(i, carry, lo=lo, size=size):
            trip(lo + i * size, size)
            return carry

        lax.fori_loop(0, ntrips, body, 0)
        lo = lo + ntrips * size


def _flash_finish(o_ref, acc_ref, heads, dv):
    dve = dv + DV_ONES
    outs = [acc_ref[h * dve:h * dve + dv, :] / acc_ref[h * dve + dv:h * dve + dv + 1, :] for h in range(heads)]
    o_ref[...] = jnp.concatenate(outs, axis=0).T.astype(o_ref.dtype)


def _values_with_ones(v, heads, dv):
    vt = v.T
    ones = jnp.ones((DV_ONES, v.shape[0]), F32)
    return jnp.concatenate([piece for h in range(heads) for piece in (vt[h * dv:(h + 1) * dv, :], ones)], axis=0)


def _mla_attn_kernel(qt_ref, k_ref, vt_ref, o_ref, m_ref, acc_ref, *, tk):
    tq = qt_ref.shape[3]
    q0 = pl.program_id(1) * tq
    nkb = (q0 + tq + tk - 1) // tk
    nfull = (q0 + 1) // tk
    key_i = lax.broadcasted_iota(jnp.int32, (tk, tq), 0)
    qry_i = q0 + lax.broadcasted_iota(jnp.int32, (tk, tq), 1)
    dve = MLA_V + DV_ONES
    _flash_init(m_ref, acc_ref)

    def kslice(kb):
        return pl.ds(pl.multiple_of(kb * tk, tk), tk)

    def scores(kb, h):
        return _dot(k_ref[0, h, kslice(kb), :], qt_ref[0, h])

    def update(kb, h, s, masked):
        if masked:
            s = jnp.where(kb * tk + key_i <= qry_i, s, MASKED)
        _flash_update(h, s, vt_ref[0, h * dve:(h + 1) * dve, kslice(kb)], m_ref, acc_ref)

    _key_block_loops(0, nfull, MLA_HEADS, scores, functools.partial(update, masked=False))
    _key_block_loops(nfull, nkb, MLA_HEADS, scores, functools.partial(update, masked=True))
    _flash_finish(o_ref, acc_ref, MLA_HEADS, MLA_V)


def _mla_attn(qt, k, vt, b, l, tq, tk):
    nq = l // tq
    hv = MLA_HEADS * MLA_V
    hve = MLA_HEADS * (MLA_V + DV_ONES)
    return pl.pallas_call(
        functools.partial(_mla_attn_kernel, tk=tk),
        grid=(b, nq),
        in_specs=[pl.BlockSpec((1, MLA_HEADS, LANES, tq), lambda bi, i: (bi, 0, 0, i)),
                  pl.BlockSpec((1, MLA_HEADS, l, LANES), lambda bi, i: (bi, 0, 0, 0), pipeline_mode=pl.Buffered(1)),
                  pl.BlockSpec((1, hve, l), lambda bi, i: (bi, 0, 0), pipeline_mode=pl.Buffered(1))],
        out_specs=pl.BlockSpec((tq, hv), lambda bi, i: (bi * nq + i, 0)),
        out_shape=jax.ShapeDtypeStruct((b * l, hv), BF16),
        scratch_shapes=[pltpu.VMEM((MLA_HEADS, tq), F32), pltpu.VMEM((hve, tq), F32)],
        compiler_params=_cparams(("parallel", "arbitrary")),
    )(qt, k, vt)


def _out_ln_kernel(a_ref, b_ref, x_ref, w_ref, g_ref, bias_ref, o_ref):
    na = a_ref.shape[1]
    mix = _dot(a_ref[...], w_ref[:na, :]) + _dot(b_ref[...], w_ref[na:, :])
    o_ref[...] = _layernorm(ALPHA * x_ref[...] + mix, g_ref[...], bias_ref[...])


def _out_ln(a, bb, x2, w, g, bias, tm=512):
    t = x2.shape[0]
    na, nb = a.shape[1], bb.shape[1]
    return pl.pallas_call(
        _out_ln_kernel,
        grid=(t // tm,),
        in_specs=[pl.BlockSpec((tm, na), lambda i: (i, 0)),
                  pl.BlockSpec((tm, nb), lambda i: (i, 0)),
                  pl.BlockSpec((tm, D_MODEL), lambda i: (i, 0)),
                  pl.BlockSpec((na + nb, D_MODEL), lambda i: (0, 0)),
                  pl.BlockSpec((1, D_MODEL), lambda i: (0, 0)),
                  pl.BlockSpec((1, D_MODEL), lambda i: (0, 0))],
        out_specs=pl.BlockSpec((tm, D_MODEL), lambda i: (i, 0)),
        out_shape=jax.ShapeDtypeStruct((t, D_MODEL), F32),
        compiler_params=_cparams(("parallel",)),
    )(a, bb, x2, w, g, bias)


def _mlp_ln_kernel(x_ref, w1_ref, w2_ref, g_ref, bias_ref, o_ref, acc_ref):
    j = pl.program_id(1)

    @pl.when(j == 0)
    def _():
        acc_ref[...] = jnp.zeros_like(acc_ref)

    hid = jnp.maximum(_dot(x_ref[...].astype(BF16), w1_ref[...]), 0.0)
    acc_ref[...] += _dot((hid * hid).astype(BF16), w2_ref[...])

    @pl.when(j == pl.num_programs(1) - 1)
    def _():
        o_ref[...] = _layernorm(ALPHA * x_ref[...] + acc_ref[...], g_ref[...], bias_ref[...])


def _mlp_ln(x2, w1, w2, g, bias, tm=1024, tf=1024):
    t = x2.shape[0]
    return pl.pallas_call(
        _mlp_ln_kernel,
        grid=(t // tm, D_FF // tf),
        in_specs=[pl.BlockSpec((tm, D_MODEL), lambda i, j: (i, 0)),
                  pl.BlockSpec((D_MODEL, tf), lambda i, j: (0, j)),
                  pl.BlockSpec((tf, D_MODEL), lambda i, j: (j, 0)),
                  pl.BlockSpec((1, D_MODEL), lambda i, j: (0, 0)),
                  pl.BlockSpec((1, D_MODEL), lambda i, j: (0, 0))],
        out_specs=pl.BlockSpec((tm, D_MODEL), lambda i, j: (i, 0)),
        out_shape=jax.ShapeDtypeStruct((t, D_MODEL), F32),
        scratch_shapes=[pltpu.VMEM((tm, D_MODEL), F32)],
        compiler_params=_cparams(("parallel", "arbitrary")),
    )(x2, w1, w2, g, bias)


def _proj1_kernel(x_ref, wb_ref, wf_ref, q_ref, qi_ref, k_ref, ki_ref, vt_ref, u_ref, wi_ref):
    xb = x_ref[...].astype(BF16)

    def cols(w_ref, start, width):
        return jnp.concatenate(
            [_dot(xb, w_ref[:, j:j + 256]) for j in range(start, start + width, 256)], axis=1)

    hd = DSA_HEADS * DSA_DH
    q_ref[...] = (cols(wb_ref, L1_Q, hd) * DSA_QSCALE).astype(BF16)
    qi_ref[...] = cols(wb_ref, L1_QI, hd).astype(BF16)
    k_ref[0] = cols(wb_ref, L1_K, hd).astype(BF16)
    ki_ref[0] = _dot(xb, wb_ref[:, L1_KI:L1_KI + LANES]).astype(BF16)
    vt_ref[0] = _values_with_ones(cols(wb_ref, L1_V, hd), DSA_HEADS, DSA_DH).astype(BF16)
    u_ref[...] = cols(wf_ref, L1_U, S5_W)
    wi_ref[...] = _dot(xb, wf_ref[:, L1_WI:L1_WI + LANES])


def _proj1(x2, wb, wf, b, l, tm=512):
    tm = min(tm, l)
    nt = l // tm
    t = b * l
    hd = DSA_HEADS * DSA_DH
    hde = DSA_HEADS * (DSA_DH + DV_ONES)
    flat = lambda bi, ti: (bi * nt + ti, 0)
    return pl.pallas_call(
        _proj1_kernel,
        grid=(b, nt),
        in_specs=[pl.BlockSpec((tm, D_MODEL), flat), _whole(wb, 2), _whole(wf, 2)],
        out_specs=[pl.BlockSpec((tm, hd), flat),
                   pl.BlockSpec((tm, hd), flat),
                   pl.BlockSpec((1, tm, hd), lambda bi, ti: (bi, ti, 0)),
                   pl.BlockSpec((1, tm, LANES), lambda bi, ti: (bi, ti, 0)),
                   pl.BlockSpec((1, hde, tm), lambda bi, ti: (bi, 0, ti)),
                   pl.BlockSpec((tm, S5_W), flat),
                   pl.BlockSpec((tm, LANES), flat)],
        out_shape=[jax.ShapeDtypeStruct((t, hd), BF16),
                   jax.ShapeDtypeStruct((t, hd), BF16),
                   jax.ShapeDtypeStruct((b, l, hd), BF16),
                   jax.ShapeDtypeStruct((b, l, LANES), BF16),
                   jax.ShapeDtypeStruct((b, hde, l), BF16),
                   jax.ShapeDtypeStruct((t, S5_W), F32),
                   jax.ShapeDtypeStruct((t, LANES), F32)],
        compiler_params=_cparams(("parallel", "parallel")),
    )(x2, wb, wf)


def _dsa_kernel(q_ref, qi_ref, wi_ref, k_ref, vt_ref, ki_ref, o_ref, key_ref, m_ref, acc_ref,
                *, topk, tk, seq):
    tq = q_ref.shape[0]
    q0 = pl.program_id(1) * tq
    nkb = (q0 + tq + tk - 1) // tk
    tc = tk
    nkc = nkb
    key_i = lax.broadcasted_iota(jnp.int32, (tk, tq), 0)
    qry_i = q0 + lax.broadcasted_iota(jnp.int32, (tk, tq), 1)
    qry_row = q0 + lax.broadcasted_iota(jnp.int32, (1, tq), 1)

    def kslice(kb):
        return pl.ds(pl.multiple_of(kb * tk, tk), tk)

    def head_rows(xt, h, width):
        per = LANES // width
        pair = xt[(h // per) * LANES:(h // per + 1) * LANES, :]
        r = lax.broadcasted_iota(jnp.int32, pair.shape, 0)
        lo = (h % per) * width
        return jnp.where((r >= lo) & (r < lo + width), pair, 0.0).astype(BF16)

    w_t = wi_ref[...].T[:IDX_HEADS, :] * (IDX_HEADS ** -0.5) * (IDX_DIM ** -0.5)
    qi_t = qi_ref[...].astype(F32).T
    zero_rows = jnp.zeros((LANES - IDX_DIM, tq), BF16)
    qi_heads = [jnp.concatenate([qi_t[h * IDX_DIM:(h + 1) * IDX_DIM, :].astype(BF16), zero_rows], axis=0)
                for h in range(IDX_HEADS)]

    def score_block(kb, carry):
        kib = ki_ref[0, kslice(kb), :]
        sc = jnp.zeros((tk, tq), F32)
        for h in range(IDX_HEADS):
            sc = sc + jnp.maximum(_dot(kib, qi_heads[h]), 0.0) * w_t[h:h + 1, :]
        sc = jnp.where(kb * tk + key_i <= qry_i, sc, -jnp.inf)
        bits = pltpu.bitcast(sc, jnp.int32)
        key_ref[kslice(kb), :] = jnp.where(bits < 0, bits ^ jnp.int32(0x7FFFFFFF), bits)
        return carry

    _blocked_loop(0, nkb, score_block, 0)

    def count(pred):
        def body(kc, acc):
            keys = key_ref[pl.ds(pl.multiple_of(kc * tc, tc), tc), :]
            hit = jnp.where(pred(keys, kc * tc), 1, 0)
            return acc + jnp.sum(hit.reshape(tc // COUNT_ROWS, COUNT_ROWS, tq), axis=0)
        acc = _blocked_loop(0, nkc, body, jnp.zeros((COUNT_ROWS, tq), jnp.int32))
        return jnp.sum(acc, axis=0, keepdims=True)

    n_nonneg = count(lambda ks, k0: ks >= 0)
    nonneg = n_nonneg >= topk
    start = (jnp.where(nonneg, 0, INT_MIN).astype(jnp.int32), jnp.where(nonneg, n_nonneg, nkc * tc))

    def bit_step(b, carry):
        prefix, n_prefix = carry
        cand = prefix | jnp.left_shift(jnp.int32(1), 30 - b)
        n_ge = count(lambda ks, k0: ks >= cand)
        take = n_ge >= topk
        return jnp.where(take, cand, prefix), jnp.where(take, n_ge, n_prefix)

    thr, n_ge_thr = lax.fori_loop(0, 31, bit_step, start)
    finite = thr != NEG_INF_KEY

    def tie_cut():
        need = topk - count(lambda ks, k0: ks > thr)
        nbits = max(1, (seq - 1).bit_length())
        key_c = lax.broadcasted_iota(jnp.int32, (tc, tq), 0)

        def step(b, ans):
            cand = ans | jnp.left_shift(jnp.int32(1), nbits - 1 - b)
            n_before = count(lambda ks, k0: (ks == thr) & (k0 + key_c < cand))
            return jnp.where(n_before < need, cand, ans)
        return lax.fori_loop(0, nbits, step, jnp.zeros((1, tq), jnp.int32))

    has_tie = jnp.max(jnp.where(finite & (n_ge_thr > topk), 1, 0)) > 0
    cut = lax.cond(has_tie, tie_cut, lambda: jnp.full((1, tq), seq, jnp.int32))
    cut = jnp.where(finite, cut, qry_row)

    def bias_block(kb, carry):
        keys = key_ref[kslice(kb), :]
        sel = (keys > thr) | ((keys == thr) & (kb * tk + key_i <= cut))
        key_ref[kslice(kb), :] = pltpu.bitcast(jnp.where(sel, jnp.inf, MASKED), jnp.int32)
        return carry

    _blocked_loop(0, nkb, bias_block, 0)

    q_t = q_ref[...].astype(F32).T
    per = LANES // DSA_DH
    dve = DSA_DH + DV_ONES
    q_heads = [head_rows(q_t, h, DSA_DH) for h in range(DSA_HEADS)]
    _flash_init(m_ref, acc_ref)

    def scores(kb, h):
        return _dot(k_ref[0, kslice(kb), (h // per) * LANES:(h // per + 1) * LANES], q_heads[h])

    def update(kb, h, s):
        ks = kslice(kb)
        cap = pltpu.bitcast(key_ref[ks, :], F32)
        _flash_update(h, jnp.minimum(s, cap), vt_ref[0, h * dve:(h + 1) * dve, ks], m_ref, acc_ref)

    _key_block_loops(0, nkb, DSA_HEADS, scores, update)
    _flash_finish(o_ref, acc_ref, DSA_HEADS, DSA_DH)


def _dsa(q, qi, wi, k, vt, ki, b, l, tq, tk):
    nq = l // tq
    topk = min(TOPK_MAX, l // 4)
    hd = DSA_HEADS * DSA_DH
    hde = DSA_HEADS * (DSA_DH + DV_ONES)
    flat = lambda bi, i: (bi * nq + i, 0)
    batch = lambda bi, i: (bi, 0, 0)
    return pl.pallas_call(
        functools.partial(_dsa_kernel, topk=topk, tk=tk, seq=l),
        grid=(b, nq),
        in_specs=[pl.BlockSpec((tq, hd), flat),
                  pl.BlockSpec((tq, hd), flat),
                  pl.BlockSpec((tq, LANES), flat),
                  pl.BlockSpec((1, l, hd), batch, pipeline_mode=pl.Buffered(1)),
                  pl.BlockSpec((1, hde, l), batch, pipeline_mode=pl.Buffered(1)),
                  pl.BlockSpec((1, l, LANES), batch, pipeline_mode=pl.Buffered(1))],
        out_specs=pl.BlockSpec((tq, hd), flat),
        out_shape=jax.ShapeDtypeStruct((b * l, hd), BF16),
        scratch_shapes=[pltpu.VMEM((l, tq), jnp.int32), pltpu.VMEM((DSA_HEADS, tq), F32),
                        pltpu.VMEM((hde, tq), F32)],
        compiler_params=_cparams(("parallel", "arbitrary")),
    )(q, qi, wi, k, vt, ki)


def _s5_kernel(u_ref, m1_ref, m2_ref, m3_ref, pa_ref, pb_ref, y_ref):
    tc = S5_TC
    rows = u_ref.shape[0] // tc
    u = jnp.concatenate([u_ref[pl.ds(s, rows, stride=tc), :] for s in range(tc)], axis=1).astype(BF16)
    x = _dot(u, m2_ref[0])
    pos = lax.broadcasted_iota(jnp.int32, x.shape, 0)
    half = x.shape[1] // 2
    d, k = 1, 0
    while d < rows:
        sh = jnp.where(pos >= d, pltpu.roll(x, d, axis=0), 0.0)
        x = x + sh * pa_ref[0, k:k + 1, :] + pltpu.roll(sh, half, axis=1) * pb_ref[0, k:k + 1, :]
        d, k = 2 * d, k + 1
    x_in = jnp.where(pos >= 1, pltpu.roll(x, 1, axis=0), 0.0)
    y = _dot(u, m1_ref[0]) + _dot(x_in.astype(BF16), m3_ref[0])
    for t in range(tc):
        y_ref[pl.ds(t, rows, stride=tc), :] = y[:, t * LANES:(t + 1) * LANES]


def _s5(u, m1, m2, m3, pa, pb, b, l):
    nblk = S5_W // LANES
    per_block = lambda a: pl.BlockSpec((1,) + a.shape[1:], lambda ci, bi: (ci,) + (0,) * (a.ndim - 1))
    return pl.pallas_call(
        _s5_kernel,
        grid=(nblk, b),
        in_specs=[pl.BlockSpec((l, LANES), lambda ci, bi: (bi, ci))] + [per_block(a) for a in (m1, m2, m3, pa, pb)],
        out_specs=pl.BlockSpec((l, LANES), lambda ci, bi: (bi, ci)),
        out_shape=jax.ShapeDtypeStruct((b * l, S5_W), F32),
        compiler_params=_cparams(("parallel", "parallel")),
    )(u, m1, m2, m3, pa, pb)


def _s5_out_kernel(y_ref, u_ref, d_ref, gw_ref, gb_ref, o_ref):
    y = jax.nn.gelu(y_ref[...] + d_ref[...] * u_ref[...])
    gate = jax.nn.sigmoid(_dot(y.astype(BF16), gw_ref[...]) + gb_ref[...])
    o_ref[...] = (y * gate).astype(o_ref.dtype)


def _s5_out(y, u, d, gw, gb, tm=1024):
    t = y.shape[0]
    tm = min(tm, t)
    return pl.pallas_call(
        _s5_out_kernel,
        grid=(t // tm,),
        in_specs=[pl.BlockSpec((tm, S5_W), lambda i: (i, 0)),
                  pl.BlockSpec((tm, S5_W), lambda i: (i, 0)),
                  _whole(d, 1), _whole(gw, 1), _whole(gb, 1)],
        out_specs=pl.BlockSpec((tm, S5_W), lambda i: (i, 0)),
        out_shape=jax.ShapeDtypeStruct((t, S5_W), BF16),
        compiler_params=_cparams(("parallel",)),
    )(y, u, d, gw, gb)


def _pad_cols(w, n):
    return jnp.pad(w, ((0, 0), (0, n - w.shape[1])))


def _l0_weights(w_in, w_uq, w_ukv):
    q, k, v, g_lr, r, c_q, c_kv, k_rope = jnp.split(
        w_in, [256, 512, 1024, 1040, 1552, 1808, 1936], axis=1)
    half = MLA_ROPE // 2
    t1, t2 = k_rope[:, :half], k_rope[:, half:]
    z = lambda n: jnp.zeros((w_in.shape[0], n), w_in.dtype)
    tail = LANES - MLA_NOPE - MLA_ROPE
    rope_a = jnp.concatenate([z(MLA_NOPE), t1, t2, z(tail)], axis=1)
    rope_b = jnp.concatenate([z(MLA_NOPE), -t2, t1, z(tail)], axis=1)
    w0 = jnp.concatenate([q, k, v, r, c_q, c_kv, _pad_cols(g_lr, LANES), rope_a, rope_b], axis=1)
    uq = w_uq.reshape(MLA_Q_RANK, MLA_HEADS, MLA_NOPE + MLA_ROPE)
    nope, r1, r2 = uq[..., :MLA_NOPE], uq[..., MLA_NOPE:MLA_NOPE + half], uq[..., MLA_NOPE + half:]
    zq = lambda n: jnp.zeros((MLA_Q_RANK, MLA_HEADS, n), w_uq.dtype)
    wqa = jnp.concatenate([nope, r1, r2, zq(tail)], axis=-1).reshape(MLA_Q_RANK, MLA_HEADS * LANES)
    wqb = jnp.concatenate([zq(MLA_NOPE), -r2, r1, zq(tail)], axis=-1).reshape(MLA_Q_RANK, MLA_HEADS * LANES)
    ukv = w_ukv.reshape(MLA_KV_RANK, MLA_HEADS, MLA_NOPE + MLA_V)
    wk = jnp.pad(ukv[..., :MLA_NOPE], ((0, 0), (0, 0), (0, LANES - MLA_NOPE))).reshape(MLA_KV_RANK, MLA_HEADS * LANES)
    wv = ukv[..., MLA_NOPE:].reshape(MLA_KV_RANK, MLA_HEADS * MLA_V)
    return w0.astype(BF16), wqa.astype(BF16), wqb.astype(BF16), wk.astype(BF16), wv.astype(BF16)


def _l1_weights(w_in):
    q, k, v, qi, ki, wi, u = jnp.split(w_in, [512, 1024, 1536, 2048, 2112, 2120], axis=1)
    wb = jnp.concatenate([q, qi, k, _pad_cols(ki, LANES), v], axis=1)
    wf = jnp.concatenate([u, _pad_cols(wi, LANES)], axis=1)
    return wb.astype(BF16), wf.astype(BF16)


def _s5_weights(a_re, a_im, b_re, b_im, c_re, c_im, log_step, nsteps):
    lam_re = jnp.minimum(a_re, -1e-4)
    lam_im = a_im
    dt = jnp.exp(log_step)[:, None]
    mag = jnp.exp(lam_re * dt)
    abar_re = mag * jnp.cos(lam_im * dt)
    abar_im = mag * jnp.sin(lam_im * dt)
    den = jnp.square(lam_re) + jnp.square(lam_im)
    nr = abar_re - 1.0
    ni = abar_im
    coef_re = (nr * lam_re + ni * lam_im) / den
    coef_im = (ni * lam_re - nr * lam_im) / den
    bbar_re = coef_re[..., None] * b_re - coef_im[..., None] * b_im
    bbar_im = coef_re[..., None] * b_im + coef_im[..., None] * b_re
    tc, gb, p, hc = S5_TC, S5_GB, S5_STATE, S5_CH
    nblk = S5_GROUPS // gb
    pr, pi = [jnp.ones_like(abar_re)], [jnp.zeros_like(abar_im)]
    for _ in range(tc):
        pr, pi = (pr + [pr[-1] * abar_re - pi[-1] * abar_im], pi + [pr[-1] * abar_im + pi[-1] * abar_re])
    pr, pi = jnp.stack(pr), jnp.stack(pi)
    ab_re = pr[:tc, :, :, None] * bbar_re - pi[:tc, :, :, None] * bbar_im
    ab_im = pr[:tc, :, :, None] * bbar_im + pi[:tc, :, :, None] * bbar_re
    eye = jnp.eye(gb, dtype=F32)[:, None, None, :, None]
    blocks = lambda a: a.reshape(a.shape[0], nblk, gb, *a.shape[2:])
    expand = lambda a: a[..., None, :] * eye
    taps = jnp.einsum('gop,dgpi->dgoi', c_re, ab_re) - jnp.einsum('gop,dgpi->dgoi', c_im, ab_im)
    lag = jnp.arange(tc)[None, :] - jnp.arange(tc)[:, None]
    k_st = jnp.where((lag >= 0)[:, :, None, None, None], taps[jnp.clip(lag, 0, tc - 1)], 0.0)
    k_st = k_st.reshape(tc, tc, nblk, gb, hc, hc).transpose(2, 0, 3, 5, 1, 4)
    m1 = expand(k_st).reshape(nblk, tc * gb * hc, tc * gb * hc)
    ab = jnp.stack([blocks(ab_re[::-1]), blocks(ab_im[::-1])])
    m2 = expand(ab.transpose(2, 1, 3, 5, 0, 4)).reshape(nblk, tc * gb * hc, 2 * gb * p)
    d_re = c_re[None] * pr[1:, :, None, :] - c_im[None] * pi[1:, :, None, :]
    d_im = c_re[None] * pi[1:, :, None, :] + c_im[None] * pr[1:, :, None, :]
    dd = jnp.stack([blocks(d_re), -blocks(d_im)])
    m3 = expand(dd.transpose(2, 0, 3, 5, 1, 4)).reshape(nblk, 2 * gb * p, tc * gb * hc)
    qr, qi, pa, pb = pr[tc].reshape(nblk, gb * p), pi[tc].reshape(nblk, gb * p), [], []
    for _ in range(nsteps):
        pa.append(jnp.concatenate([qr, qr], axis=-1))
        pb.append(jnp.concatenate([-qi, qi], axis=-1))
        qr, qi = qr * qr - qi * qi, 2.0 * qr * qi
    return m1.astype(BF16), m2.astype(BF16), m3.astype(BF16), jnp.stack(pa, axis=1), jnp.stack(pb, axis=1)


def kernel(x, positions, l0_w_in, l0_gla_wg2, l0_gla_bg, l0_gla_norm, l0_mla_q_norm, l0_mla_w_uq, l0_mla_kv_norm, l0_mla_w_ukv, l0_w_out, l0_ln1_g, l0_ln1_b, l0_mlp_w1, l0_mlp_w2, l0_ln2_g, l0_ln2_b, l1_w_in, l1_s5_a_re, l1_s5_a_im, l1_s5_b_re, l1_s5_b_im, l1_s5_c_re, l1_s5_c_im, l1_s5_d, l1_s5_log_step, l1_glu_w, l1_glu_b, l1_w_out, l1_ln1_g, l1_ln1_b, l1_mlp_w1, l1_mlp_w2, l1_ln2_g, l1_ln2_b):
    b, l, _ = x.shape
    t = b * l
    row = lambda a: a.reshape(1, -1)
    x2 = x.reshape(t, D_MODEL)
    tq = min(256, l)
    tk = min(512, l)

    w0, wqa, wqb, wk, wv = _l0_weights(l0_w_in, l0_mla_w_uq, l0_mla_w_ukv)
    h0 = _proj(x2, w0, F32)
    o_gla = _gla(h0, l0_gla_wg2.astype(BF16), row(l0_gla_bg), row(l0_gla_norm), b, l)
    tm = min(512, l)
    half = MLA_ROPE // 2
    inv_freq = ROPE_THETA ** (-jnp.arange(half, dtype=F32) / half)
    invf = jnp.concatenate([inv_freq, inv_freq]).reshape(MLA_ROPE, 1)
    pos3 = positions.reshape(t // tm, 1, tm)
    qt, km, vt = _mla_proj(h0, pos3, invf, row(l0_mla_q_norm), row(l0_mla_kv_norm), wqa, wqb, wk, wv, b, l, tm)
    o_mla = _mla_attn(qt, km, vt, b, l, tq, tk)
    x2 = _out_ln(o_gla, o_mla, x2, l0_w_out.astype(BF16), row(l0_ln1_g), row(l0_ln1_b))
    x2 = _mlp_ln(x2, l0_mlp_w1.astype(BF16), l0_mlp_w2.astype(BF16), row(l0_ln2_g), row(l0_ln2_b))

    w1b, w1f = _l1_weights(l1_w_in)
    q1, qi1, k1, ki1, vt1, u1, wi1 = _proj1(x2, w1b, w1f, b, l)
    o_dsa = _dsa(q1, qi1, wi1, k1, vt1, ki1, b, l, tq, tk)
    seg = l // S5_TC
    s5w = _s5_weights(l1_s5_a_re, l1_s5_a_im, l1_s5_b_re, l1_s5_b_im, l1_s5_c_re, l1_s5_c_im, l1_s5_log_step,
                      max(1, (seg - 1).bit_length()))
    y1 = _s5(u1, *s5w, b, l)
    o_s5 = _s5_out(y1, u1, row(l1_s5_d), l1_glu_w.astype(BF16), row(l1_glu_b))
    x2 = _out_ln(o_dsa, o_s5, x2, l1_w_out.astype(BF16), row(l1_ln1_g), row(l1_ln1_b))
    x2 = _mlp_ln(x2, l1_mlp_w1.astype(BF16), l1_mlp_w2.astype(BF16), row(l1_ln2_g), row(l1_ln2_b))
    return x2.reshape(b, l, D_MODEL)
```

```python
import functools
import math

import jax
import jax.numpy as jnp
from jax import lax
from jax.experimental import pallas as pl
from jax.experimental.pallas import tpu as pltpu

BF16 = jnp.bfloat16
F32 = jnp.float32

D_MODEL = 1024
DEPTH = 2
GLA_HEADS, GLA_DK, GLA_DV, GLA_RANK, GLA_TAU, GLA_CHUNK = 4, 64, 128, 16, 16.0, 64
MLA_HEADS, MLA_Q_RANK, MLA_KV_RANK, MLA_NOPE, MLA_ROPE, MLA_V = 8, 256, 128, 64, 32, 64
ROPE_THETA = 10000.0
DSA_HEADS, DSA_DH, IDX_HEADS, IDX_DIM, TOPK_MAX = 8, 64, 8, 64, 256
S5_GROUPS, S5_CH, S5_STATE = 32, 16, 64
S5_W = S5_GROUPS * S5_CH
S5_TC = 8
S5_GB = 8
D_FF = 4 * D_MODEL
LN_EPS = 1e-5
ALPHA = (2 * DEPTH) ** 0.25

LANES = 128
SUBLANES = 8
VMEM_LIMIT = 56 * 1024 * 1024
MASKED = -1e30
INT_MIN = -(2 ** 31)
NEG_INF_KEY = INT_MIN + 0x7FFFFF
LOG2E = math.log2(math.e)
SCORES_AHEAD = 3
BLOCKS_PER_TRIP = (4, 2, 1)
COUNT_ROWS = 32
DV_ONES = 16
MLA_QSCALE = (MLA_NOPE + MLA_ROPE) ** -0.5 * LOG2E
DSA_QSCALE = DSA_DH ** -0.5 * LOG2E

L0_Q, L0_K, L0_V, L0_R, L0_CQ, L0_CKV, L0_MISC, L0_ROPE_A, L0_ROPE_B, L0_N = (
    0, 256, 512, 1024, 1536, 1792, 1920, 2048, 2176, 2304)
L1_Q, L1_QI, L1_K, L1_KI, L1_V, L1_NB = 0, 512, 1024, 1536, 1664, 2176
L1_U, L1_WI, L1_NF = 0, 512, 640


def _cparams(sem):
    return pltpu.CompilerParams(dimension_semantics=sem, vmem_limit_bytes=VMEM_LIMIT)


def _dot(a, b):
    return jnp.dot(a, b, preferred_element_type=F32)


def _dot_nt(a, b):
    return lax.dot_general(a, b, (((1,), (1,)), ((), ())), preferred_element_type=F32)


def _layernorm(v, g, b):
    mu = jnp.mean(v, -1, keepdims=True)
    d = v - mu
    var = jnp.mean(d * d, -1, keepdims=True)
    return d * lax.rsqrt(var + LN_EPS) * g + b


def _rmsnorm(v, g):
    return v * lax.rsqrt(jnp.mean(v * v, -1, keepdims=True) + LN_EPS) * g


def _whole(a, nargs):
    return pl.BlockSpec(a.shape, lambda *_: (0,) * a.ndim)


def _proj_kernel(x_ref, w_ref, o_ref, *, chunk):
    xb = x_ref[...].astype(BF16)
    for j in range(0, o_ref.shape[1], chunk):
        o_ref[:, j:j + chunk] = _dot(xb, w_ref[:, j:j + chunk]).astype(o_ref.dtype)


def _proj(x2, w, out_dtype, tm=512):
    t, k = x2.shape
    n = w.shape[1]
    return pl.pallas_call(
        functools.partial(_proj_kernel, chunk=256),
        grid=(t // tm,),
        in_specs=[pl.BlockSpec((tm, k), lambda i: (i, 0)),
                  pl.BlockSpec((k, n), lambda i: (0, 0))],
        out_specs=pl.BlockSpec((tm, n), lambda i: (i, 0)),
        out_shape=jax.ShapeDtypeStruct((t, n), out_dtype),
        compiler_params=_cparams(("parallel",)),
    )(x2, w)


def _gla_kernel(q_ref, k_ref, v_ref, r_ref, misc_ref, wg2_ref, bg_ref, gn_ref, o_ref, st_ref):
    c = GLA_CHUNK
    rows = q_ref.shape[0]

    @pl.when(pl.program_id(1) == 0)
    def _():
        st_ref[...] = jnp.zeros_like(st_ref)

    z = _dot(misc_ref[:, :GLA_RANK].astype(BF16), wg2_ref[...]) + bg_ref[...]
    log_a = jax.nn.log_sigmoid(z) * (1.0 / GLA_TAU)
    row = lax.broadcasted_iota(jnp.int32, (rows, rows), 0)
    col = lax.broadcasted_iota(jnp.int32, (rows, rows), 1)
    tril = jnp.where((col <= row) & (col >= row - row % c), 1.0, 0.0).astype(BF16)
    a_hi = log_a.astype(BF16)
    rem = log_a - a_hi.astype(F32)
    a_mid = rem.astype(BF16)
    a_lo = (rem - a_mid.astype(F32)).astype(BF16)
    cum = _dot(tril, a_hi) + _dot(tril, a_mid) + _dot(tril, a_lo)
    k = k_ref[...]
    q_dec = (q_ref[...] * (GLA_DK ** -0.5) * jnp.exp(cum)).astype(BF16)
    k_inv = (k * jnp.exp(-cum)).astype(BF16)
    causal = (lax.broadcasted_iota(jnp.int32, (c, c), 1) <= lax.broadcasted_iota(jnp.int32, (c, c), 0))
    for j in range(rows // c):
        rs = slice(j * c, (j + 1) * c)
        cum_last = cum[(j + 1) * c - 1:(j + 1) * c, :]
        k_end = (k[rs, :] * jnp.exp(cum_last - cum[rs, :])).astype(BF16)
        dec = jnp.exp(cum_last)
        for h in range(GLA_HEADS):
            ks = slice(h * GLA_DK, (h + 1) * GLA_DK)
            vs = slice(h * GLA_DV, (h + 1) * GLA_DV)
            v = v_ref[rs, vs]
            att = jnp.where(causal, _dot_nt(q_dec[rs, ks], k_inv[rs, ks]), 0.0)
            st = st_ref[h]
            o = _dot(att.astype(BF16), v.astype(BF16)) + _dot_nt(q_dec[rs, ks], st.astype(BF16))
            st_ref[h] = st * dec[:, ks] + _dot(v.T.astype(BF16), k_end[:, ks])
            o = _rmsnorm(o, gn_ref[:, vs])
            r = r_ref[rs, vs]
            o_ref[rs, vs] = (o * (r * jax.nn.sigmoid(r))).astype(o_ref.dtype)


def _gla(h0, wg2, bg, gn, b, l, chunks_per_step=4):
    c = GLA_CHUNK * min(chunks_per_step, l // GLA_CHUNK)
    nc = l // c
    hk = GLA_HEADS * GLA_DK
    hv = GLA_HEADS * GLA_DV

    def rows(j):
        return lambda bi, ci: (bi * nc + ci, j)

    return pl.pallas_call(
        _gla_kernel,
        grid=(b, nc),
        in_specs=[pl.BlockSpec((c, hk), rows(L0_Q // hk)),
                  pl.BlockSpec((c, hk), rows(L0_K // hk)),
                  pl.BlockSpec((c, hv), rows(L0_V // hv)),
                  pl.BlockSpec((c, hv), rows(L0_R // hv)),
                  pl.BlockSpec((c, LANES), rows(L0_MISC // LANES)),
                  pl.BlockSpec((GLA_RANK, hk), lambda bi, ci: (0, 0)),
                  pl.BlockSpec((1, hk), lambda bi, ci: (0, 0)),
                  pl.BlockSpec((1, hv), lambda bi, ci: (0, 0))],
        out_specs=pl.BlockSpec((c, hv), rows(0)),
        out_shape=jax.ShapeDtypeStruct((b * l, hv), BF16),
        scratch_shapes=[pltpu.VMEM((GLA_HEADS, GLA_DV, GLA_DK), F32)],
        compiler_params=_cparams(("parallel", "arbitrary")),
    )(h0, h0, h0, h0, h0, wg2, bg, gn)


def _mla_proj_kernel(cq_ref, ckv_ref, ra_ref, rb_ref, pos_ref, invf_ref, qn_ref, kvn_ref,
                     wqa_ref, wqb_ref, wk_ref, wv_ref, qt_ref, k_ref, vt_ref):
    tm = cq_ref.shape[0]
    pos = pos_ref[0].astype(F32)
    ang = invf_ref[...] * pos
    ones = jnp.ones((MLA_NOPE, tm), F32)
    pad = jnp.zeros((LANES - MLA_NOPE - MLA_ROPE, tm), F32)
    cos = jnp.concatenate([ones, jnp.cos(ang), pad], axis=0).T
    sin = jnp.concatenate([jnp.zeros((MLA_NOPE, tm), F32), jnp.sin(ang), pad], axis=0).T
    cqn = _rmsnorm(cq_ref[...], qn_ref[...]).astype(BF16)
    ckvn = _rmsnorm(ckv_ref[...], kvn_ref[...]).astype(BF16)
    k_rot = ra_ref[...] * cos + rb_ref[...] * sin
    for h in range(MLA_HEADS):
        hs = slice(h * LANES, (h + 1) * LANES)
        qh = _dot(cqn, wqa_ref[:, hs]) * cos + _dot(cqn, wqb_ref[:, hs]) * sin
        qt_ref[0, h] = (qh * MLA_QSCALE).T.astype(qt_ref.dtype)
        k_ref[0, h] = (_dot(ckvn, wk_ref[:, hs]) + k_rot).astype(k_ref.dtype)
    vt_ref[0] = _values_with_ones(_dot(ckvn, wv_ref[...]), MLA_HEADS, MLA_V).astype(vt_ref.dtype)


def _mla_proj(h0, pos3, invf, qn, kvn, wqa, wqb, wk, wv, b, l, tm):
    nt = l // tm

    def rows(j):
        return lambda bi, ti: (bi * nt + ti, j)

    consts = (invf, qn, kvn, wqa, wqb, wk, wv)
    return pl.pallas_call(
        _mla_proj_kernel,
        grid=(b, nt),
        in_specs=[pl.BlockSpec((tm, MLA_Q_RANK), rows(L0_CQ // MLA_Q_RANK)),
                  pl.BlockSpec((tm, MLA_KV_RANK), rows(L0_CKV // MLA_KV_RANK)),
                  pl.BlockSpec((tm, LANES), rows(L0_ROPE_A // LANES)),
                  pl.BlockSpec((tm, LANES), rows(L0_ROPE_B // LANES)),
                  pl.BlockSpec((1, 1, tm), lambda bi, ti: (bi * nt + ti, 0, 0))]
                 + [_whole(a, 2) for a in consts],
        out_specs=[pl.BlockSpec((1, MLA_HEADS, LANES, tm), lambda bi, ti: (bi, 0, 0, ti)),
                   pl.BlockSpec((1, MLA_HEADS, tm, LANES), lambda bi, ti: (bi, 0, ti, 0)),
                   pl.BlockSpec((1, MLA_HEADS * (MLA_V + DV_ONES), tm), lambda bi, ti: (bi, 0, ti))],
        out_shape=[jax.ShapeDtypeStruct((b, MLA_HEADS, LANES, l), BF16),
                   jax.ShapeDtypeStruct((b, MLA_HEADS, l, LANES), BF16),
                   jax.ShapeDtypeStruct((b, MLA_HEADS * (MLA_V + DV_ONES), l), BF16)],
        compiler_params=_cparams(("parallel", "parallel")),
    )(h0, h0, h0, h0, pos3, *consts)


def _flash_init(m_ref, acc_ref):
    m_ref[...] = jnp.full_like(m_ref, MASKED)
    acc_ref[...] = jnp.zeros_like(acc_ref)


def _flash_update(h, s, vt, m_ref, acc_ref):
    dve = vt.shape[0]
    hr = slice(h, h + 1)
    rows = slice(h * dve, (h + 1) * dve)
    m_old = m_ref[hr, :]
    m_new = jnp.maximum(m_old, jnp.max(s, axis=0, keepdims=True))
    p = jnp.exp2(s - m_new)
    acc_ref[rows, :] = jnp.exp2(m_old - m_new) * acc_ref[rows, :] + _dot(vt, p.astype(BF16))
    m_ref[hr, :] = m_new


def _blocked_loop(lo, hi, body, carry):
    for size in BLOCKS_PER_TRIP:
        ntrips = (hi - lo) // size

        def trip(i, c, lo=lo, size=size):
            for j in range(size):
                c = body(lo + i * size + j, c)
            return c

        carry = lax.fori_loop(0, ntrips, trip, carry)
        lo = lo + ntrips * size
    return carry


def _key_block_loops(lo, hi, heads, scores, update):
    def trip(first, nblocks):
        steps = [(first + j, h) for j in range(nblocks) for h in range(heads)]
        ahead = [scores(*st) for st in steps[:SCORES_AHEAD]]
        for i, (kb, h) in enumerate(steps):
            s = ahead.pop(0)
            if i + SCORES_AHEAD < len(steps):
                ahead.append(scores(*steps[i + SCORES_AHEAD]))
            update(kb, h, s)

    for size in BLOCKS_PER_TRIP:
        ntrips = (hi - lo) // size

        def body(i, carry, lo=lo, size=size):
            trip(lo + i * size, size)
            return carry

        lax.fori_loop(0, ntrips, body, 0)
        lo = lo + ntrips * size


def _flash_finish(o_ref, acc_ref, heads, dv):
    dve = dv + DV_ONES
    outs = [acc_ref[h * dve:h * dve + dv, :] / acc_ref[h * dve + dv:h * dve + dv + 1, :] for h in range(heads)]
    o_ref[...] = jnp.concatenate(outs, axis=0).T.astype(o_ref.dtype)


def _values_with_ones(v, heads, dv):
    vt = v.T
    ones = jnp.ones((DV_ONES, v.shape[0]), F32)
    return jnp.concatenate([piece for h in range(heads) for piece in (vt[h * dv:(h + 1) * dv, :], ones)], axis=0)


def _mla_attn_kernel(qt_ref, k_ref, vt_ref, o_ref, m_ref, acc_ref, *, tk):
    tq = qt_ref.shape[3]
    q0 = pl.program_id(1) * tq
    nkb = (q0 + tq + tk - 1) // tk
    nfull = (q0 + 1) // tk
    key_i = lax.broadcasted_iota(jnp.int32, (tk, tq), 0)
    qry_i = q0 + lax.broadcasted_iota(jnp.int32, (tk, tq), 1)
    dve = MLA_V + DV_ONES
    _flash_init(m_ref, acc_ref)

    def kslice(kb):
        return pl.ds(pl.multiple_of(kb * tk, tk), tk)

    def scores(kb, h):
        return _dot(k_ref[0, h, kslice(kb), :], qt_ref[0, h])

    def update(kb, h, s, masked):
        if masked:
            s = jnp.where(kb * tk + key_i <= qry_i, s, MASKED)
        _flash_update(h, s, vt_ref[0, h * dve:(h + 1) * dve, kslice(kb)], m_ref, acc_ref)

    _key_block_loops(0, nfull, MLA_HEADS, scores, functools.partial(update, masked=False))
    _key_block_loops(nfull, nkb, MLA_HEADS, scores, functools.partial(update, masked=True))
    _flash_finish(o_ref, acc_ref, MLA_HEADS, MLA_V)


def _mla_attn(qt, k, vt, b, l, tq, tk):
    nq = l // tq
    hv = MLA_HEADS * MLA_V
    hve = MLA_HEADS * (MLA_V + DV_ONES)
    return pl.pallas_call(
        functools.partial(_mla_attn_kernel, tk=tk),
        grid=(b, nq),
        in_specs=[pl.BlockSpec((1, MLA_HEADS, LANES, tq), lambda bi, i: (bi, 0, 0, i)),
                  pl.BlockSpec((1, MLA_HEADS, l, LANES), lambda bi, i: (bi, 0, 0, 0), pipeline_mode=pl.Buffered(1)),
                  pl.BlockSpec((1, hve, l), lambda bi, i: (bi, 0, 0), pipeline_mode=pl.Buffered(1))],
        out_specs=pl.BlockSpec((tq, hv), lambda bi, i: (bi * nq + i, 0)),
        out_shape=jax.ShapeDtypeStruct((b * l, hv), BF16),
        scratch_shapes=[pltpu.VMEM((MLA_HEADS, tq), F32), pltpu.VMEM((hve, tq), F32)],
        compiler_params=_cparams(("parallel", "arbitrary")),
    )(qt, k, vt)


def _out_ln_kernel(a_ref, b_ref, x_ref, w_ref, g_ref, bias_ref, o_ref):
    na = a_ref.shape[1]
    mix = _dot(a_ref[...], w_ref[:na, :]) + _dot(b_ref[...], w_ref[na:, :])
    o_ref[...] = _layernorm(ALPHA * x_ref[...] + mix, g_ref[...], bias_ref[...])


def _out_ln(a, bb, x2, w, g, bias, tm=512):
    t = x2.shape[0]
    na, nb = a.shape[1], bb.shape[1]
    return pl.pallas_call(
        _out_ln_kernel,
        grid=(t // tm,),
        in_specs=[pl.BlockSpec((tm, na), lambda i: (i, 0)),
                  pl.BlockSpec((tm, nb), lambda i: (i, 0)),
                  pl.BlockSpec((tm, D_MODEL), lambda i: (i, 0)),
                  pl.BlockSpec((na + nb, D_MODEL), lambda i: (0, 0)),
                  pl.BlockSpec((1, D_MODEL), lambda i: (0, 0)),
                  pl.BlockSpec((1, D_MODEL), lambda i: (0, 0))],
        out_specs=pl.BlockSpec((tm, D_MODEL), lambda i: (i, 0)),
        out_shape=jax.ShapeDtypeStruct((t, D_MODEL), F32),
        compiler_params=_cparams(("parallel",)),
    )(a, bb, x2, w, g, bias)


def _mlp_ln_kernel(x_ref, w1_ref, w2_ref, g_ref, bias_ref, o_ref, acc_ref):
    j = pl.program_id(1)

    @pl.when(j == 0)
    def _():
        acc_ref[...] = jnp.zeros_like(acc_ref)

    hid = jnp.maximum(_dot(x_ref[...].astype(BF16), w1_ref[...]), 0.0)
    acc_ref[...] += _dot((hid * hid).astype(BF16), w2_ref[...])

    @pl.when(j == pl.num_programs(1) - 1)
    def _():
        o_ref[...] = _layernorm(ALPHA * x_ref[...] + acc_ref[...], g_ref[...], bias_ref[...])


def _mlp_ln(x2, w1, w2, g, bias, tm=1024, tf=1024):
    t = x2.shape[0]
    return pl.pallas_call(
        _mlp_ln_kernel,
        grid=(t // tm, D_FF // tf),
        in_specs=[pl.BlockSpec((tm, D_MODEL), lambda i, j: (i, 0)),
                  pl.BlockSpec((D_MODEL, tf), lambda i, j: (0, j)),
                  pl.BlockSpec((tf, D_MODEL), lambda i, j: (j, 0)),
                  pl.BlockSpec((1, D_MODEL), lambda i, j: (0, 0)),
                  pl.BlockSpec((1, D_MODEL), lambda i, j: (0, 0))],
        out_specs=pl.BlockSpec((tm, D_MODEL), lambda i, j: (i, 0)),
        out_shape=jax.ShapeDtypeStruct((t, D_MODEL), F32),
        scratch_shapes=[pltpu.VMEM((tm, D_MODEL), F32)],
        compiler_params=_cparams(("parallel", "arbitrary")),
    )(x2, w1, w2, g, bias)


def _proj1_kernel(x_ref, wb_ref, wf_ref, q_ref, qi_ref, k_ref, ki_ref, vt_ref, u_ref, wi_ref):
    xb = x_ref[...].astype(BF16)

    def cols(w_ref, start, width):
        return jnp.concatenate(
            [_dot(xb, w_ref[:, j:j + 256]) for j in range(start, start + width, 256)], axis=1)

    hd = DSA_HEADS * DSA_DH
    q_ref[...] = (cols(wb_ref, L1_Q, hd) * DSA_QSCALE).astype(BF16)
    qi_ref[...] = cols(wb_ref, L1_QI, hd).astype(BF16)
    k_ref[0] = cols(wb_ref, L1_K, hd).astype(BF16)
    ki_ref[0] = _dot(xb, wb_ref[:, L1_KI:L1_KI + LANES]).astype(BF16)
    vt_ref[0] = _values_with_ones(cols(wb_ref, L1_V, hd), DSA_HEADS, DSA_DH).astype(BF16)
    u_ref[...] = cols(wf_ref, L1_U, S5_W)
    wi_ref[...] = _dot(xb, wf_ref[:, L1_WI:L1_WI + LANES])


def _proj1(x2, wb, wf, b, l, tm=512):
    tm = min(tm, l)
    nt = l // tm
    t = b * l
    hd = DSA_HEADS * DSA_DH
    hde = DSA_HEADS * (DSA_DH + DV_ONES)
    flat = lambda bi, ti: (bi * nt + ti, 0)
    return pl.pallas_call(
        _proj1_kernel,
        grid=(b, nt),
        in_specs=[pl.BlockSpec((tm, D_MODEL), flat), _whole(wb, 2), _whole(wf, 2)],
        out_specs=[pl.BlockSpec((tm, hd), flat),
                   pl.BlockSpec((tm, hd), flat),
                   pl.BlockSpec((1, tm, hd), lambda bi, ti: (bi, ti, 0)),
                   pl.BlockSpec((1, tm, LANES), lambda bi, ti: (bi, ti, 0)),
                   pl.BlockSpec((1, hde, tm), lambda bi, ti: (bi, 0, ti)),
                   pl.BlockSpec((tm, S5_W), flat),
                   pl.BlockSpec((tm, LANES), flat)],
        out_shape=[jax.ShapeDtypeStruct((t, hd), BF16),
                   jax.ShapeDtypeStruct((t, hd), BF16),
                   jax.ShapeDtypeStruct((b, l, hd), BF16),
                   jax.ShapeDtypeStruct((b, l, LANES), BF16),
                   jax.ShapeDtypeStruct((b, hde, l), BF16),
                   jax.ShapeDtypeStruct((t, S5_W), F32),
                   jax.ShapeDtypeStruct((t, LANES), F32)],
        compiler_params=_cparams(("parallel", "parallel")),
    )(x2, wb, wf)


def _dsa_kernel(q_ref, qi_ref, wi_ref, k_ref, vt_ref, ki_ref, o_ref, key_ref, m_ref, acc_ref,
                *, topk, tk, seq):
    tq = q_ref.shape[0]
    q0 = pl.program_id(1) * tq
    nkb = (q0 + tq + tk - 1) // tk
    key_i = lax.broadcasted_iota(jnp.int32, (tk, tq), 0)
    qry_i = q0 + lax.broadcasted_iota(jnp.int32, (tk, tq), 1)
    qry_row = q0 + lax.broadcasted_iota(jnp.int32, (1, tq), 1)

    def kslice(kb):
        return pl.ds(pl.multiple_of(kb * tk, tk), tk)

    def head_rows(xt, h, width):
        per = LANES // width
        pair = xt[(h // per) * LANES:(h // per + 1) * LANES, :]
        r = lax.broadcasted_iota(jnp.int32, pair.shape, 0)
        lo = (h % per) * width
        return jnp.where((r >= lo) & (r < lo + width), pair, 0.0).astype(BF16)

    w_t = wi_ref[...].T[:IDX_HEADS, :] * (IDX_HEADS ** -0.5) * (IDX_DIM ** -0.5)
    qi_t = qi_ref[...].astype(F32).T
    zero_rows = jnp.zeros((LANES - IDX_DIM, tq), BF16)
    qi_heads = [jnp.concatenate([qi_t[h * IDX_DIM:(h + 1) * IDX_DIM, :].astype(BF16), zero_rows], axis=0)
                for h in range(IDX_HEADS)]

    def partial_counts(hit):
        return jnp.sum(jnp.where(hit, 1, 0).reshape(tk // COUNT_ROWS, COUNT_ROWS, tq), axis=0)

    def score_block(kb, n_nonneg):
        kib = ki_ref[0, kslice(kb), :]
        sc = jnp.zeros((tk, tq), F32)
        for h in range(IDX_HEADS):
            sc = sc + jnp.maximum(_dot(kib, qi_heads[h]), 0.0) * w_t[h:h + 1, :]
        sc = jnp.where(kb * tk + key_i <= qry_i, sc, -jnp.inf)
        bits = pltpu.bitcast(sc, jnp.int32)
        key_ref[kslice(kb), :] = jnp.where(bits < 0, bits ^ jnp.int32(0x7FFFFFFF), bits)
        return n_nonneg + partial_counts(bits >= 0)

    n_nonneg = _blocked_loop(0, nkb, score_block, jnp.zeros((COUNT_ROWS, tq), jnp.int32))
    n_nonneg = jnp.sum(n_nonneg, axis=0, keepdims=True)

    def count(pred):
        def body(kc, acc):
            return acc + partial_counts(pred(key_ref[kslice(kc), :], kc * tk))
        acc = _blocked_loop(0, nkb, body, jnp.zeros((COUNT_ROWS, tq), jnp.int32))
        return jnp.sum(acc, axis=0, keepdims=True)

    nonneg = n_nonneg >= topk
    start = (jnp.where(nonneg, 0, INT_MIN).astype(jnp.int32), jnp.where(nonneg, n_nonneg, nkb * tk))

    def bit_step(b, carry):
        prefix, n_prefix = carry
        cand = prefix | jnp.left_shift(jnp.int32(1), 30 - b)
        n_ge = count(lambda ks, k0: ks >= cand)
        take = n_ge >= topk
        return jnp.where(take, cand, prefix), jnp.where(take, n_ge, n_prefix)

    thr, n_ge_thr = lax.fori_loop(0, 31, bit_step, start)
    finite = thr != NEG_INF_KEY

    def tie_cut():
        need = topk - count(lambda ks, k0: ks > thr)
        nbits = max(1, (seq - 1).bit_length())

        def step(b, ans):
            cand = ans | jnp.left_shift(jnp.int32(1), nbits - 1 - b)
            n_before = count(lambda ks, k0: (ks == thr) & (k0 + key_i < cand))
            return jnp.where(n_before < need, cand, ans)
        return lax.fori_loop(0, nbits, step, jnp.zeros((1, tq), jnp.int32))

    has_tie = jnp.max(jnp.where(finite & (n_ge_thr > topk), 1, 0)) > 0
    cut = lax.cond(has_tie, tie_cut, lambda: jnp.full((1, tq), seq, jnp.int32))
    cut = jnp.where(finite, cut, qry_row)

    def bias_block(kb, carry):
        keys = key_ref[kslice(kb), :]
        sel = (keys > thr) | ((keys == thr) & (kb * tk + key_i <= cut))
        key_ref[kslice(kb), :] = pltpu.bitcast(jnp.where(sel, jnp.inf, MASKED), jnp.int32)
        return carry

    _blocked_loop(0, nkb, bias_block, 0)

    q_t = q_ref[...].astype(F32).T
    per = LANES // DSA_DH
    dve = DSA_DH + DV_ONES
    q_heads = [head_rows(q_t, h, DSA_DH) for h in range(DSA_HEADS)]
    _flash_init(m_ref, acc_ref)

    def scores(kb, h):
        return _dot(k_ref[0, kslice(kb), (h // per) * LANES:(h // per + 1) * LANES], q_heads[h])

    def update(kb, h, s):
        ks = kslice(kb)
        cap = pltpu.bitcast(key_ref[ks, :], F32)
        _flash_update(h, jnp.minimum(s, cap), vt_ref[0, h * dve:(h + 1) * dve, ks], m_ref, acc_ref)

    _key_block_loops(0, nkb, DSA_HEADS, scores, update)
    _flash_finish(o_ref, acc_ref, DSA_HEADS, DSA_DH)


def _dsa(q, qi, wi, k, vt, ki, b, l, tq, tk):
    nq = l // tq
    topk = min(TOPK_MAX, l // 4)
    hd = DSA_HEADS * DSA_DH
    hde = DSA_HEADS * (DSA_DH + DV_ONES)
    flat = lambda bi, i: (bi * nq + i, 0)
    batch = lambda bi, i: (bi, 0, 0)
    return pl.pallas_call(
        functools.partial(_dsa_kernel, topk=topk, tk=tk, seq=l),
        grid=(b, nq),
        in_specs=[pl.BlockSpec((tq, hd), flat),
                  pl.BlockSpec((tq, hd), flat),
                  pl.BlockSpec((tq, LANES), flat),
                  pl.BlockSpec((1, l, hd), batch, pipeline_mode=pl.Buffered(1)),
                  pl.BlockSpec((1, hde, l), batch, pipeline_mode=pl.Buffered(1)),
                  pl.BlockSpec((1, l, LANES), batch, pipeline_mode=pl.Buffered(1))],
        out_specs=pl.BlockSpec((tq, hd), flat),
        out_shape=jax.ShapeDtypeStruct((b * l, hd), BF16),
        scratch_shapes=[pltpu.VMEM((l, tq), jnp.int32), pltpu.VMEM((DSA_HEADS, tq), F32),
                        pltpu.VMEM((hde, tq), F32)],
        compiler_params=_cparams(("parallel", "arbitrary")),
    )(q, qi, wi, k, vt, ki)


def _s5_kernel(u_ref, m1_ref, m2_ref, m3_ref, pa_ref, pb_ref, y_ref):
    tc = S5_TC
    rows = u_ref.shape[0] // tc
    u = jnp.concatenate([u_ref[pl.ds(s, rows, stride=tc), :] for s in range(tc)], axis=1).astype(BF16)
    x = _dot(u, m2_ref[0])
    pos = lax.broadcasted_iota(jnp.int32, x.shape, 0)
    half = x.shape[1] // 2
    d, k = 1, 0
    while d < rows:
        sh = jnp.where(pos >= d, pltpu.roll(x, d, axis=0), 0.0)
        x = x + sh * pa_ref[0, k:k + 1, :] + pltpu.roll(sh, half, axis=1) * pb_ref[0, k:k + 1, :]
        d, k = 2 * d, k + 1
    x_in = jnp.where(pos >= 1, pltpu.roll(x, 1, axis=0), 0.0)
    y = _dot(u, m1_ref[0]) + _dot(x_in.astype(BF16), m3_ref[0])
    for t in range(tc):
        y_ref[pl.ds(t, rows, stride=tc), :] = y[:, t * LANES:(t + 1) * LANES]


def _s5(u, m1, m2, m3, pa, pb, b, l):
    nblk = S5_W // LANES
    per_block = lambda a: pl.BlockSpec((1,) + a.shape[1:], lambda ci, bi: (ci,) + (0,) * (a.ndim - 1))
    return pl.pallas_call(
        _s5_kernel,
        grid=(nblk, b),
        in_specs=[pl.BlockSpec((l, LANES), lambda ci, bi: (bi, ci))] + [per_block(a) for a in (m1, m2, m3, pa, pb)],
        out_specs=pl.BlockSpec((l, LANES), lambda ci, bi: (bi, ci)),
        out_shape=jax.ShapeDtypeStruct((b * l, S5_W), F32),
        compiler_params=_cparams(("parallel", "parallel")),
    )(u, m1, m2, m3, pa, pb)


def _s5_out_kernel(y_ref, u_ref, d_ref, gw_ref, gb_ref, o_ref):
    y = jax.nn.gelu(y_ref[...] + d_ref[...] * u_ref[...])
    gate = jax.nn.sigmoid(_dot(y.astype(BF16), gw_ref[...]) + gb_ref[...])
    o_ref[...] = (y * gate).astype(o_ref.dtype)


def _s5_out(y, u, d, gw, gb, tm=1024):
    t = y.shape[0]
    tm = min(tm, t)
    return pl.pallas_call(
        _s5_out_kernel,
        grid=(t // tm,),
        in_specs=[pl.BlockSpec((tm, S5_W), lambda i: (i, 0)),
                  pl.BlockSpec((tm, S5_W), lambda i: (i, 0)),
                  _whole(d, 1), _whole(gw, 1), _whole(gb, 1)],
        out_specs=pl.BlockSpec((tm, S5_W), lambda i: (i, 0)),
        out_shape=jax.ShapeDtypeStruct((t, S5_W), BF16),
        compiler_params=_cparams(("parallel",)),
    )(y, u, d, gw, gb)


def _pad_cols(w, n):
    return jnp.pad(w, ((0, 0), (0, n - w.shape[1])))


def _l0_weights(w_in, w_uq, w_ukv):
    q, k, v, g_lr, r, c_q, c_kv, k_rope = jnp.split(
        w_in, [256, 512, 1024, 1040, 1552, 1808, 1936], axis=1)
    half = MLA_ROPE // 2
    t1, t2 = k_rope[:, :half], k_rope[:, half:]
    z = lambda n: jnp.zeros((w_in.shape[0], n), w_in.dtype)
    tail = LANES - MLA_NOPE - MLA_ROPE
    rope_a = jnp.concatenate([z(MLA_NOPE), t1, t2, z(tail)], axis=1)
    rope_b = jnp.concatenate([z(MLA_NOPE), -t2, t1, z(tail)], axis=1)
    w0 = jnp.concatenate([q, k, v, r, c_q, c_kv, _pad_cols(g_lr, LANES), rope_a, rope_b], axis=1)
    uq = w_uq.reshape(MLA_Q_RANK, MLA_HEADS, MLA_NOPE + MLA_ROPE)
    nope, r1, r2 = uq[..., :MLA_NOPE], uq[..., MLA_NOPE:MLA_NOPE + half], uq[..., MLA_NOPE + half:]
    zq = lambda n: jnp.zeros((MLA_Q_RANK, MLA_HEADS, n), w_uq.dtype)
    wqa = jnp.concatenate([nope, r1, r2, zq(tail)], axis=-1).reshape(MLA_Q_RANK, MLA_HEADS * LANES)
    wqb = jnp.concatenate([zq(MLA_NOPE), -r2, r1, zq(tail)], axis=-1).reshape(MLA_Q_RANK, MLA_HEADS * LANES)
    ukv = w_ukv.reshape(MLA_KV_RANK, MLA_HEADS, MLA_NOPE + MLA_V)
    wk = jnp.pad(ukv[..., :MLA_NOPE], ((0, 0), (0, 0), (0, LANES - MLA_NOPE))).reshape(MLA_KV_RANK, MLA_HEADS * LANES)
    wv = ukv[..., MLA_NOPE:].reshape(MLA_KV_RANK, MLA_HEADS * MLA_V)
    return w0.astype(BF16), wqa.astype(BF16), wqb.astype(BF16), wk.astype(BF16), wv.astype(BF16)


def _l1_weights(w_in):
    q, k, v, qi, ki, wi, u = jnp.split(w_in, [512, 1024, 1536, 2048, 2112, 2120], axis=1)
    wb = jnp.concatenate([q, qi, k, _pad_cols(ki, LANES), v], axis=1)
    wf = jnp.concatenate([u, _pad_cols(wi, LANES)], axis=1)
    return wb.astype(BF16), wf.astype(BF16)


def _s5_weights(a_re, a_im, b_re, b_im, c_re, c_im, log_step, nsteps):
    lam_re = jnp.minimum(a_re, -1e-4)
    lam_im = a_im
    dt = jnp.exp(log_step)[:, None]
    mag = jnp.exp(lam_re * dt)
    abar_re = mag * jnp.cos(lam_im * dt)
    abar_im = mag * jnp.sin(lam_im * dt)
    den = jnp.square(lam_re) + jnp.square(lam_im)
    nr = abar_re - 1.0
    ni = abar_im
    coef_re = (nr * lam_re + ni * lam_im) / den
    coef_im = (ni * lam_re - nr * lam_im) / den
    bbar_re = coef_re[..., None] * b_re - coef_im[..., None] * b_im
    bbar_im = coef_re[..., None] * b_im + coef_im[..., None] * b_re
    tc, gb, p, hc = S5_TC, S5_GB, S5_STATE, S5_CH
    nblk = S5_GROUPS // gb
    pr, pi = [jnp.ones_like(abar_re)], [jnp.zeros_like(abar_im)]
    for _ in range(tc):
        pr, pi = (pr + [pr[-1] * abar_re - pi[-1] * abar_im], pi + [pr[-1] * abar_im + pi[-1] * abar_re])
    pr, pi = jnp.stack(pr), jnp.stack(pi)
    ab_re = pr[:tc, :, :, None] * bbar_re - pi[:tc, :, :, None] * bbar_im
    ab_im = pr[:tc, :, :, None] * bbar_im + pi[:tc, :, :, None] * bbar_re
    blocks = lambda a: a.reshape(a.shape[0], nblk, gb, *a.shape[2:])

    def expand(a, rw):
        y, z = a.shape[-2:]
        a = a.reshape(nblk, -1, y * z)
        src = jnp.arange(y * z)[:, None]
        dst = jnp.arange(y * gb * z)[None, :]
        spread = ((dst // (gb * z) == src // z) & (dst % z == src % z)).astype(F32)
        wide = jnp.einsum('brc,cd->brd', a, spread, precision=lax.Precision.HIGHEST)
        row_g = (jnp.arange(a.shape[1])[:, None] // rw) % gb
        return jnp.where(row_g == (dst // z) % gb, wide, 0.0)
    taps = jnp.einsum('gop,dgpi->dgoi', c_re, ab_re) - jnp.einsum('gop,dgpi->dgoi', c_im, ab_im)
    lag = jnp.arange(tc)[None, :] - jnp.arange(tc)[:, None]
    k_st = jnp.where((lag >= 0)[:, :, None, None, None], taps[jnp.clip(lag, 0, tc - 1)], 0.0)
    k_st = k_st.reshape(tc, tc, nblk, gb, hc, hc).transpose(2, 0, 3, 5, 1, 4)
    m1 = expand(k_st, hc)
    ab = jnp.stack([blocks(ab_re[::-1]), blocks(ab_im[::-1])])
    m2 = expand(ab.transpose(2, 1, 3, 5, 0, 4), hc)
    d_re = c_re[None] * pr[1:, :, None, :] - c_im[None] * pi[1:, :, None, :]
    d_im = c_re[None] * pi[1:, :, None, :] + c_im[None] * pr[1:, :, None, :]
    dd = jnp.stack([blocks(d_re), -blocks(d_im)])
    m3 = expand(dd.transpose(2, 0, 3, 5, 1, 4), p)
    qr, qi, pa, pb = pr[tc].reshape(nblk, gb * p), pi[tc].reshape(nblk, gb * p), [], []
    for _ in range(nsteps):
        pa.append(jnp.concatenate([qr, qr], axis=-1))
        pb.append(jnp.concatenate([-qi, qi], axis=-1))
        qr, qi = qr * qr - qi * qi, 2.0 * qr * qi
    return m1.astype(BF16), m2.astype(BF16), m3.astype(BF16), jnp.stack(pa, axis=1), jnp.stack(pb, axis=1)


def kernel(x, positions, l0_w_in, l0_gla_wg2, l0_gla_bg, l0_gla_norm, l0_mla_q_norm, l0_mla_w_uq, l0_mla_kv_norm, l0_mla_w_ukv, l0_w_out, l0_ln1_g, l0_ln1_b, l0_mlp_w1, l0_mlp_w2, l0_ln2_g, l0_ln2_b, l1_w_in, l1_s5_a_re, l1_s5_a_im, l1_s5_b_re, l1_s5_b_im, l1_s5_c_re, l1_s5_c_im, l1_s5_d, l1_s5_log_step, l1_glu_w, l1_glu_b, l1_w_out, l1_ln1_g, l1_ln1_b, l1_mlp_w1, l1_mlp_w2, l1_ln2_g, l1_ln2_b):
    b, l, _ = x.shape
    t = b * l
    row = lambda a: a.reshape(1, -1)
    x2 = x.reshape(t, D_MODEL)
    tq = min(256, l)
    tk = min(512, l)

    w0, wqa, wqb, wk, wv = _l0_weights(l0_w_in, l0_mla_w_uq, l0_mla_w_ukv)
    h0 = _proj(x2, w0, F32)
    o_gla = _gla(h0, l0_gla_wg2.astype(BF16), row(l0_gla_bg), row(l0_gla_norm), b, l)
    tm = min(512, l)
    half = MLA_ROPE // 2
    inv_freq = ROPE_THETA ** (-jnp.arange(half, dtype=F32) / half)
    invf = jnp.concatenate([inv_freq, inv_freq]).reshape(MLA_ROPE, 1)
    pos3 = positions.reshape(t // tm, 1, tm)
    qt, km, vt = _mla_proj(h0, pos3, invf, row(l0_mla_q_norm), row(l0_mla_kv_norm), wqa, wqb, wk, wv, b, l, tm)
    o_mla = _mla_attn(qt, km, vt, b, l, tq, tk)
    x2 = _out_ln(o_gla, o_mla, x2, l0_w_out.astype(BF16), row(l0_ln1_g), row(l0_ln1_b))
    x2 = _mlp_ln(x2, l0_mlp_w1.astype(BF16), l0_mlp_w2.astype(BF16), row(l0_ln2_g), row(l0_ln2_b))

    w1b, w1f = _l1_weights(l1_w_in)
    q1, qi1, k1, ki1, vt1, u1, wi1 = _proj1(x2, w1b, w1f, b, l)
    o_dsa = _dsa(q1, qi1, wi1, k1, vt1, ki1, b, l, tq, tk)
    seg = l // S5_TC
    s5w = _s5_weights(l1_s5_a_re, l1_s5_a_im, l1_s5_b_re, l1_s5_b_im, l1_s5_c_re, l1_s5_c_im, l1_s5_log_step,
                      max(1, (seg - 1).bit_length()))
    y1 = _s5(u1, *s5w, b, l)
    o_s5 = _s5_out(y1, u1, row(l1_s5_d), l1_glu_w.astype(BF16), row(l1_glu_b))
    x2 = _out_ln(o_dsa, o_s5, x2, l1_w_out.astype(BF16), row(l1_ln1_g), row(l1_ln1_b))
    x2 = _mlp_ln(x2, l1_mlp_w1.astype(BF16), l1_mlp_w2.astype(BF16), row(l1_ln2_g), row(l1_ln2_b))
    return x2.reshape(b, l, D_MODEL)
```

```python
import functools
import math

import jax
import jax.numpy as jnp
from jax import lax
from jax.experimental import pallas as pl
from jax.experimental.pallas import tpu as pltpu

BF16 = jnp.bfloat16
F32 = jnp.float32

D_MODEL = 1024
DEPTH = 2
GLA_HEADS, GLA_DK, GLA_DV, GLA_RANK, GLA_TAU, GLA_CHUNK = 4, 64, 128, 16, 16.0, 64
MLA_HEADS, MLA_Q_RANK, MLA_KV_RANK, MLA_NOPE, MLA_ROPE, MLA_V = 8, 256, 128, 64, 32, 64
ROPE_THETA = 10000.0
DSA_HEADS, DSA_DH, IDX_HEADS, IDX_DIM, TOPK_MAX = 8, 64, 8, 64, 256
S5_GROUPS, S5_CH, S5_STATE = 32, 16, 64
S5_W = S5_GROUPS * S5_CH
S5_TC = 8
S5_GB = 8
D_FF = 4 * D_MODEL
LN_EPS = 1e-5
ALPHA = (2 * DEPTH) ** 0.25

LANES = 128
SUBLANES = 8
VMEM_LIMIT = 56 * 1024 * 1024
MASKED = -1e30
INT_MIN = -(2 ** 31)
NEG_INF_KEY = INT_MIN + 0x7FFFFF
LOG2E = math.log2(math.e)
SCORES_AHEAD = 4
BLOCKS_PER_TRIP = (4, 2, 1)
COUNT_ROWS = 32
DV_ONES = 16
MLA_QSCALE = (MLA_NOPE + MLA_ROPE) ** -0.5 * LOG2E
DSA_QSCALE = DSA_DH ** -0.5 * LOG2E

L0_Q, L0_K, L0_V, L0_R, L0_CQ, L0_CKV, L0_MISC, L0_ROPE_A, L0_ROPE_B, L0_N = (
    0, 256, 512, 1024, 1536, 1792, 1920, 2048, 2176, 2304)
L1_Q, L1_QI, L1_K, L1_KI, L1_V, L1_NB = 0, 512, 1024, 1536, 1664, 2176
L1_U, L1_WI, L1_NF = 0, 512, 640


def _cparams(sem):
    return pltpu.CompilerParams(dimension_semantics=sem, vmem_limit_bytes=VMEM_LIMIT)


def _dot(a, b):
    return jnp.dot(a, b, preferred_element_type=F32)


def _dot_nt(a, b):
    return lax.dot_general(a, b, (((1,), (1,)), ((), ())), preferred_element_type=F32)


def _layernorm(v, g, b):
    mu = jnp.mean(v, -1, keepdims=True)
    d = v - mu
    var = jnp.mean(d * d, -1, keepdims=True)
    return d * lax.rsqrt(var + LN_EPS) * g + b


def _rmsnorm(v, g):
    return v * lax.rsqrt(jnp.mean(v * v, -1, keepdims=True) + LN_EPS) * g


def _whole(a, nargs):
    return pl.BlockSpec(a.shape, lambda *_: (0,) * a.ndim)


def _proj_kernel(x_ref, w_ref, o_ref, *, chunk):
    xb = x_ref[...].astype(BF16)
    for j in range(0, o_ref.shape[1], chunk):
        o_ref[:, j:j + chunk] = _dot(xb, w_ref[:, j:j + chunk]).astype(o_ref.dtype)


def _proj(x2, w, out_dtype, tm=512):
    t, k = x2.shape
    n = w.shape[1]
    return pl.pallas_call(
        functools.partial(_proj_kernel, chunk=256),
        grid=(t // tm,),
        in_specs=[pl.BlockSpec((tm, k), lambda i: (i, 0)),
                  pl.BlockSpec((k, n), lambda i: (0, 0))],
        out_specs=pl.BlockSpec((tm, n), lambda i: (i, 0)),
        out_shape=jax.ShapeDtypeStruct((t, n), out_dtype),
        compiler_params=_cparams(("parallel",)),
    )(x2, w)


def _gla_kernel(q_ref, k_ref, v_ref, r_ref, misc_ref, wg2_ref, bg_ref, gn_ref, o_ref, st_ref):
    c = GLA_CHUNK
    rows = q_ref.shape[0]

    @pl.when(pl.program_id(1) == 0)
    def _():
        st_ref[...] = jnp.zeros_like(st_ref)

    z = _dot(misc_ref[:, :GLA_RANK].astype(BF16), wg2_ref[...]) + bg_ref[...]
    log_a = jax.nn.log_sigmoid(z) * (1.0 / GLA_TAU)
    row = lax.broadcasted_iota(jnp.int32, (rows, rows), 0)
    col = lax.broadcasted_iota(jnp.int32, (rows, rows), 1)
    tril = jnp.where((col <= row) & (col >= row - row % c), 1.0, 0.0).astype(BF16)
    a_hi = log_a.astype(BF16)
    rem = log_a - a_hi.astype(F32)
    a_mid = rem.astype(BF16)
    a_lo = (rem - a_mid.astype(F32)).astype(BF16)
    cum = _dot(tril, a_hi) + _dot(tril, a_mid) + _dot(tril, a_lo)
    k = k_ref[...]
    q_dec = (q_ref[...] * (GLA_DK ** -0.5) * jnp.exp(cum)).astype(BF16)
    k_inv = (k * jnp.exp(-cum)).astype(BF16)
    causal = (lax.broadcasted_iota(jnp.int32, (c, c), 1) <= lax.broadcasted_iota(jnp.int32, (c, c), 0))
    for j in range(rows // c):
        rs = slice(j * c, (j + 1) * c)
        cum_last = cum[(j + 1) * c - 1:(j + 1) * c, :]
        k_end = (k[rs, :] * jnp.exp(cum_last - cum[rs, :])).astype(BF16)
        dec = jnp.exp(cum_last)
        for h in range(GLA_HEADS):
            ks = slice(h * GLA_DK, (h + 1) * GLA_DK)
            vs = slice(h * GLA_DV, (h + 1) * GLA_DV)
            v = v_ref[rs, vs]
            att = jnp.where(causal, _dot_nt(q_dec[rs, ks], k_inv[rs, ks]), 0.0)
            st = st_ref[h]
            o = _dot(att.astype(BF16), v.astype(BF16)) + _dot_nt(q_dec[rs, ks], st.astype(BF16))
            st_ref[h] = st * dec[:, ks] + _dot(v.T.astype(BF16), k_end[:, ks])
            o = _rmsnorm(o, gn_ref[:, vs])
            r = r_ref[rs, vs]
            o_ref[rs, vs] = (o * (r * jax.nn.sigmoid(r))).astype(o_ref.dtype)


def _gla(h0, wg2, bg, gn, b, l, chunks_per_step=4):
    c = GLA_CHUNK * min(chunks_per_step, l // GLA_CHUNK)
    nc = l // c
    hk = GLA_HEADS * GLA_DK
    hv = GLA_HEADS * GLA_DV

    def rows(j):
        return lambda bi, ci: (bi * nc + ci, j)

    return pl.pallas_call(
        _gla_kernel,
        grid=(b, nc),
        in_specs=[pl.BlockSpec((c, hk), rows(L0_Q // hk)),
                  pl.BlockSpec((c, hk), rows(L0_K // hk)),
                  pl.BlockSpec((c, hv), rows(L0_V // hv)),
                  pl.BlockSpec((c, hv), rows(L0_R // hv)),
                  pl.BlockSpec((c, LANES), rows(L0_MISC // LANES)),
                  pl.BlockSpec((GLA_RANK, hk), lambda bi, ci: (0, 0)),
                  pl.BlockSpec((1, hk), lambda bi, ci: (0, 0)),
                  pl.BlockSpec((1, hv), lambda bi, ci: (0, 0))],
        out_specs=pl.BlockSpec((c, hv), rows(0)),
        out_shape=jax.ShapeDtypeStruct((b * l, hv), BF16),
        scratch_shapes=[pltpu.VMEM((GLA_HEADS, GLA_DV, GLA_DK), F32)],
        compiler_params=_cparams(("parallel", "arbitrary")),
    )(h0, h0, h0, h0, h0, wg2, bg, gn)


def _mla_proj_kernel(cq_ref, ckv_ref, ra_ref, rb_ref, pos_ref, invf_ref, qn_ref, kvn_ref,
                     wqa_ref, wqb_ref, wk_ref, wv_ref, qt_ref, k_ref, vt_ref):
    tm = cq_ref.shape[0]
    pos = pos_ref[0].astype(F32)
    ang = invf_ref[...] * pos
    ones = jnp.ones((MLA_NOPE, tm), F32)
    pad = jnp.zeros((LANES - MLA_NOPE - MLA_ROPE, tm), F32)
    cos = jnp.concatenate([ones, jnp.cos(ang), pad], axis=0).T
    sin = jnp.concatenate([jnp.zeros((MLA_NOPE, tm), F32), jnp.sin(ang), pad], axis=0).T
    cqn = _rmsnorm(cq_ref[...], qn_ref[...]).astype(BF16)
    ckvn = _rmsnorm(ckv_ref[...], kvn_ref[...]).astype(BF16)
    k_rot = ra_ref[...] * cos + rb_ref[...] * sin
    for h in range(MLA_HEADS):
        hs = slice(h * LANES, (h + 1) * LANES)
        qh = _dot(cqn, wqa_ref[:, hs]) * cos + _dot(cqn, wqb_ref[:, hs]) * sin
        qt_ref[0, h] = (qh * MLA_QSCALE).T.astype(qt_ref.dtype)
        k_ref[0, h] = (_dot(ckvn, wk_ref[:, hs]) + k_rot).astype(k_ref.dtype)
    vt_ref[0] = _values_with_ones(_dot(ckvn, wv_ref[...]), MLA_HEADS, MLA_V).astype(vt_ref.dtype)


def _mla_proj(h0, pos3, invf, qn, kvn, wqa, wqb, wk, wv, b, l, tm):
    nt = l // tm

    def rows(j):
        return lambda bi, ti: (bi * nt + ti, j)

    consts = (invf, qn, kvn, wqa, wqb, wk, wv)
    return pl.pallas_call(
        _mla_proj_kernel,
        grid=(b, nt),
        in_specs=[pl.BlockSpec((tm, MLA_Q_RANK), rows(L0_CQ // MLA_Q_RANK)),
                  pl.BlockSpec((tm, MLA_KV_RANK), rows(L0_CKV // MLA_KV_RANK)),
                  pl.BlockSpec((tm, LANES), rows(L0_ROPE_A // LANES)),
                  pl.BlockSpec((tm, LANES), rows(L0_ROPE_B // LANES)),
                  pl.BlockSpec((1, 1, tm), lambda bi, ti: (bi * nt + ti, 0, 0))]
                 + [_whole(a, 2) for a in consts],
        out_specs=[pl.BlockSpec((1, MLA_HEADS, LANES, tm), lambda bi, ti: (bi, 0, 0, ti)),
                   pl.BlockSpec((1, MLA_HEADS, tm, LANES), lambda bi, ti: (bi, 0, ti, 0)),
                   pl.BlockSpec((1, MLA_HEADS * (MLA_V + DV_ONES), tm), lambda bi, ti: (bi, 0, ti))],
        out_shape=[jax.ShapeDtypeStruct((b, MLA_HEADS, LANES, l), BF16),
                   jax.ShapeDtypeStruct((b, MLA_HEADS, l, LANES), BF16),
                   jax.ShapeDtypeStruct((b, MLA_HEADS * (MLA_V + DV_ONES), l), BF16)],
        compiler_params=_cparams(("parallel", "parallel")),
    )(h0, h0, h0, h0, pos3, *consts)


def _flash_init(m_ref, acc_ref):
    m_ref[...] = jnp.full_like(m_ref, MASKED)
    acc_ref[...] = jnp.zeros_like(acc_ref)


def _flash_update(h, s, vt, m_ref, acc_ref):
    dve = vt.shape[0]
    hr = slice(h, h + 1)
    rows = slice(h * dve, (h + 1) * dve)
    m_old = m_ref[hr, :]
    m_new = jnp.maximum(m_old, jnp.max(s, axis=0, keepdims=True))
    p = jnp.exp2(s - m_new)
    acc_ref[rows, :] = jnp.exp2(m_old - m_new) * acc_ref[rows, :] + _dot(vt, p.astype(BF16))
    m_ref[hr, :] = m_new


def _blocked_loop(lo, hi, body, carry):
    for size in BLOCKS_PER_TRIP:
        ntrips = (hi - lo) // size

        def trip(i, c, lo=lo, size=size):
            for j in range(size):
                c = body(lo + i * size + j, c)
            return c

        carry = lax.fori_loop(0, ntrips, trip, carry)
        lo = lo + ntrips * size
    return carry


def _key_block_loops(lo, hi, heads, scores, update):
    def trip(first, nblocks):
        steps = [(first + j, h) for j in range(nblocks) for h in range(heads)]
        ahead = [scores(*st) for st in steps[:SCORES_AHEAD]]
        for i, (kb, h) in enumerate(steps):
            s = ahead.pop(0)
            if i + SCORES_AHEAD < len(steps):
                ahead.append(scores(*steps[i + SCORES_AHEAD]))
            update(kb, h, s)

    for size in BLOCKS_PER_TRIP:
        ntrips = (hi - lo) // size

        def body(i, carry, lo=lo, size=size):
            trip(lo + i * size, size)
            return carry

        lax.fori_loop(0, ntrips, body, 0)
        lo = lo + ntrips * size


def _flash_finish(o_ref, acc_ref, heads, dv):
    dve = dv + DV_ONES
    outs = [acc_ref[h * dve:h * dve + dv, :] / acc_ref[h * dve + dv:h * dve + dv + 1, :] for h in range(heads)]
    o_ref[...] = jnp.concatenate(outs, axis=0).T.astype(o_ref.dtype)


def _values_with_ones(v, heads, dv):
    vt = v.T
    ones = jnp.ones((DV_ONES, v.shape[0]), F32)
    return jnp.concatenate([piece for h in range(heads) for piece in (vt[h * dv:(h + 1) * dv, :], ones)], axis=0)


def _mla_attn_kernel(qt_ref, k_ref, vt_ref, o_ref, m_ref, acc_ref, *, tk):
    tq = qt_ref.shape[3]
    q0 = pl.program_id(1) * tq
    nkb = (q0 + tq + tk - 1) // tk
    nfull = (q0 + 1) // tk
    key_i = lax.broadcasted_iota(jnp.int32, (tk, tq), 0)
    qry_i = q0 + lax.broadcasted_iota(jnp.int32, (tk, tq), 1)
    dve = MLA_V + DV_ONES
    _flash_init(m_ref, acc_ref)

    def kslice(kb):
        return pl.ds(pl.multiple_of(kb * tk, tk), tk)

    def scores(kb, h):
        return _dot(k_ref[0, h, kslice(kb), :], qt_ref[0, h])

    def update(kb, h, s, masked):
        if masked:
            s = jnp.where(kb * tk + key_i <= qry_i, s, MASKED)
        _flash_update(h, s, vt_ref[0, h * dve:(h + 1) * dve, kslice(kb)], m_ref, acc_ref)

    _key_block_loops(0, nfull, MLA_HEADS, scores, functools.partial(update, masked=False))
    _key_block_loops(nfull, nkb, MLA_HEADS, scores, functools.partial(update, masked=True))
    _flash_finish(o_ref, acc_ref, MLA_HEADS, MLA_V)


def _mla_attn(qt, k, vt, b, l, tq, tk):
    nq = l // tq
    hv = MLA_HEADS * MLA_V
    hve = MLA_HEADS * (MLA_V + DV_ONES)
    return pl.pallas_call(
        functools.partial(_mla_attn_kernel, tk=tk),
        grid=(b, nq),
        in_specs=[pl.BlockSpec((1, MLA_HEADS, LANES, tq), lambda bi, i: (bi, 0, 0, i)),
                  pl.BlockSpec((1, MLA_HEADS, l, LANES), lambda bi, i: (bi, 0, 0, 0), pipeline_mode=pl.Buffered(1)),
                  pl.BlockSpec((1, hve, l), lambda bi, i: (bi, 0, 0), pipeline_mode=pl.Buffered(1))],
        out_specs=pl.BlockSpec((tq, hv), lambda bi, i: (bi * nq + i, 0)),
        out_shape=jax.ShapeDtypeStruct((b * l, hv), BF16),
        scratch_shapes=[pltpu.VMEM((MLA_HEADS, tq), F32), pltpu.VMEM((hve, tq), F32)],
        compiler_params=_cparams(("parallel", "arbitrary")),
    )(qt, k, vt)


def _out_ln_kernel(a_ref, b_ref, x_ref, w_ref, g_ref, bias_ref, o_ref):
    na = a_ref.shape[1]
    mix = _dot(a_ref[...], w_ref[:na, :]) + _dot(b_ref[...], w_ref[na:, :])
    o_ref[...] = _layernorm(ALPHA * x_ref[...] + mix, g_ref[...], bias_ref[...])


def _out_ln(a, bb, x2, w, g, bias, tm=512):
    t = x2.shape[0]
    na, nb = a.shape[1], bb.shape[1]
    return pl.pallas_call(
        _out_ln_kernel,
        grid=(t // tm,),
        in_specs=[pl.BlockSpec((tm, na), lambda i: (i, 0)),
                  pl.BlockSpec((tm, nb), lambda i: (i, 0)),
                  pl.BlockSpec((tm, D_MODEL), lambda i: (i, 0)),
                  pl.BlockSpec((na + nb, D_MODEL), lambda i: (0, 0)),
                  pl.BlockSpec((1, D_MODEL), lambda i: (0, 0)),
                  pl.BlockSpec((1, D_MODEL), lambda i: (0, 0))],
        out_specs=pl.BlockSpec((tm, D_MODEL), lambda i: (i, 0)),
        out_shape=jax.ShapeDtypeStruct((t, D_MODEL), F32),
        compiler_params=_cparams(("parallel",)),
    )(a, bb, x2, w, g, bias)


def _mlp_ln_kernel(x_ref, w1_ref, w2_ref, g_ref, bias_ref, o_ref, acc_ref):
    j = pl.program_id(1)

    @pl.when(j == 0)
    def _():
        acc_ref[...] = jnp.zeros_like(acc_ref)

    hid = jnp.maximum(_dot(x_ref[...].astype(BF16), w1_ref[...]), 0.0)
    acc_ref[...] += _dot((hid * hid).astype(BF16), w2_ref[...])

    @pl.when(j == pl.num_programs(1) - 1)
    def _():
        o_ref[...] = _layernorm(ALPHA * x_ref[...] + acc_ref[...], g_ref[...], bias_ref[...])


def _mlp_ln(x2, w1, w2, g, bias, tm=1024, tf=1024):
    t = x2.shape[0]
    return pl.pallas_call(
        _mlp_ln_kernel,
        grid=(t // tm, D_FF // tf),
        in_specs=[pl.BlockSpec((tm, D_MODEL), lambda i, j: (i, 0)),
                  pl.BlockSpec((D_MODEL, tf), lambda i, j: (0, j)),
                  pl.BlockSpec((tf, D_MODEL), lambda i, j: (j, 0)),
                  pl.BlockSpec((1, D_MODEL), lambda i, j: (0, 0)),
                  pl.BlockSpec((1, D_MODEL), lambda i, j: (0, 0))],
        out_specs=pl.BlockSpec((tm, D_MODEL), lambda i, j: (i, 0)),
        out_shape=jax.ShapeDtypeStruct((t, D_MODEL), F32),
        scratch_shapes=[pltpu.VMEM((tm, D_MODEL), F32)],
        compiler_params=_cparams(("parallel", "arbitrary")),
    )(x2, w1, w2, g, bias)


def _proj1_kernel(x_ref, wb_ref, wf_ref, q_ref, qi_ref, k_ref, ki_ref, vt_ref, u_ref, wi_ref):
    xb = x_ref[...].astype(BF16)

    def cols(w_ref, start, width):
        return jnp.concatenate(
            [_dot(xb, w_ref[:, j:j + 256]) for j in range(start, start + width, 256)], axis=1)

    hd = DSA_HEADS * DSA_DH
    q_ref[...] = (cols(wb_ref, L1_Q, hd) * DSA_QSCALE).astype(BF16)
    qi_ref[...] = cols(wb_ref, L1_QI, hd).astype(BF16)
    k_ref[0] = cols(wb_ref, L1_K, hd).astype(BF16)
    ki_ref[0] = _dot(xb, wb_ref[:, L1_KI:L1_KI + LANES]).astype(BF16)
    vt_ref[0] = _values_with_ones(cols(wb_ref, L1_V, hd), DSA_HEADS, DSA_DH).astype(BF16)
    u_ref[...] = cols(wf_ref, L1_U, S5_W)
    wi_ref[...] = _dot(xb, wf_ref[:, L1_WI:L1_WI + LANES])


def _proj1(x2, wb, wf, b, l, tm=512):
    tm = min(tm, l)
    nt = l // tm
    t = b * l
    hd = DSA_HEADS * DSA_DH
    hde = DSA_HEADS * (DSA_DH + DV_ONES)
    flat = lambda bi, ti: (bi * nt + ti, 0)
    return pl.pallas_call(
        _proj1_kernel,
        grid=(b, nt),
        in_specs=[pl.BlockSpec((tm, D_MODEL), flat), _whole(wb, 2), _whole(wf, 2)],
        out_specs=[pl.BlockSpec((tm, hd), flat),
                   pl.BlockSpec((tm, hd), flat),
                   pl.BlockSpec((1, tm, hd), lambda bi, ti: (bi, ti, 0)),
                   pl.BlockSpec((1, tm, LANES), lambda bi, ti: (bi, ti, 0)),
                   pl.BlockSpec((1, hde, tm), lambda bi, ti: (bi, 0, ti)),
                   pl.BlockSpec((tm, S5_W), flat),
                   pl.BlockSpec((tm, LANES), flat)],
        out_shape=[jax.ShapeDtypeStruct((t, hd), BF16),
                   jax.ShapeDtypeStruct((t, hd), BF16),
                   jax.ShapeDtypeStruct((b, l, hd), BF16),
                   jax.ShapeDtypeStruct((b, l, LANES), BF16),
                   jax.ShapeDtypeStruct((b, hde, l), BF16),
                   jax.ShapeDtypeStruct((t, S5_W), F32),
                   jax.ShapeDtypeStruct((t, LANES), F32)],
        compiler_params=_cparams(("parallel", "parallel")),
    )(x2, wb, wf)


def _dsa_kernel(q_ref, qi_ref, wi_ref, k_ref, vt_ref, ki_ref, o_ref, key_ref, m_ref, acc_ref,
                *, topk, tk, seq):
    tq = q_ref.shape[0]
    q0 = pl.program_id(1) * tq
    nkb = (q0 + tq + tk - 1) // tk
    key_i = lax.broadcasted_iota(jnp.int32, (tk, tq), 0)
    qry_i = q0 + lax.broadcasted_iota(jnp.int32, (tk, tq), 1)
    qry_row = q0 + lax.broadcasted_iota(jnp.int32, (1, tq), 1)

    def kslice(kb):
        return pl.ds(pl.multiple_of(kb * tk, tk), tk)

    def head_rows(xt, h, width):
        per = LANES // width
        pair = xt[(h // per) * LANES:(h // per + 1) * LANES, :]
        r = lax.broadcasted_iota(jnp.int32, pair.shape, 0)
        lo = (h % per) * width
        return jnp.where((r >= lo) & (r < lo + width), pair, 0.0).astype(BF16)

    w_t = wi_ref[...].T[:IDX_HEADS, :] * (IDX_HEADS ** -0.5) * (IDX_DIM ** -0.5)
    qi_t = qi_ref[...].astype(F32).T
    zero_rows = jnp.zeros((LANES - IDX_DIM, tq), BF16)
    qi_heads = [jnp.concatenate([qi_t[h * IDX_DIM:(h + 1) * IDX_DIM, :].astype(BF16), zero_rows], axis=0)
                for h in range(IDX_HEADS)]

    def partial_counts(hit):
        return jnp.sum(jnp.where(hit, 1, 0).reshape(tk // COUNT_ROWS, COUNT_ROWS, tq), axis=0)

    def score_block(kb, n_nonneg):
        kib = ki_ref[0, kslice(kb), :]
        sc = jnp.zeros((tk, tq), F32)
        for h in range(IDX_HEADS):
            sc = sc + jnp.maximum(_dot(kib, qi_heads[h]), 0.0) * w_t[h:h + 1, :]
        sc = jnp.where(kb * tk + key_i <= qry_i, sc, -jnp.inf)
        bits = pltpu.bitcast(sc, jnp.int32)
        key_ref[kslice(kb), :] = jnp.where(bits < 0, bits ^ jnp.int32(0x7FFFFFFF), bits)
        return n_nonneg + partial_counts(bits >= 0)

    n_nonneg = _blocked_loop(0, nkb, score_block, jnp.zeros((COUNT_ROWS, tq), jnp.int32))
    n_nonneg = jnp.sum(n_nonneg, axis=0, keepdims=True)

    def count(pred):
        def body(kc, acc):
            return acc + partial_counts(pred(key_ref[kslice(kc), :], kc * tk))
        acc = _blocked_loop(0, nkb, body, jnp.zeros((COUNT_ROWS, tq), jnp.int32))
        return jnp.sum(acc, axis=0, keepdims=True)

    nonneg = n_nonneg >= topk
    start = (jnp.where(nonneg, 0, INT_MIN).astype(jnp.int32), jnp.where(nonneg, n_nonneg, nkb * tk))

    def bit_step(b, carry):
        prefix, n_prefix = carry
        cand = prefix | jnp.left_shift(jnp.int32(1), 30 - b)
        n_ge = count(lambda ks, k0: ks >= cand)
        take = n_ge >= topk
        return jnp.where(take, cand, prefix), jnp.where(take, n_ge, n_prefix)

    thr, n_ge_thr = lax.fori_loop(0, 31, bit_step, start)
    finite = thr != NEG_INF_KEY

    def tie_cut():
        need = topk - count(lambda ks, k0: ks > thr)
        nbits = max(1, (seq - 1).bit_length())

        def step(b, ans):
            cand = ans | jnp.left_shift(jnp.int32(1), nbits - 1 - b)
            n_before = count(lambda ks, k0: (ks == thr) & (k0 + key_i < cand))
            return jnp.where(n_before < need, cand, ans)
        return lax.fori_loop(0, nbits, step, jnp.zeros((1, tq), jnp.int32))

    has_tie = jnp.max(jnp.where(finite & (n_ge_thr > topk), 1, 0)) > 0
    cut = lax.cond(has_tie, tie_cut, lambda: jnp.full((1, tq), seq, jnp.int32))
    cut = jnp.where(finite, cut, qry_row)

    def bias_block(kb, carry):
        keys = key_ref[kslice(kb), :]
        sel = (keys > thr) | ((keys == thr) & (kb * tk + key_i <= cut))
        key_ref[kslice(kb), :] = pltpu.bitcast(jnp.where(sel, jnp.inf, MASKED), jnp.int32)
        return carry

    _blocked_loop(0, nkb, bias_block, 0)

    q_t = q_ref[...].astype(F32).T
    per = LANES // DSA_DH
    dve = DSA_DH + DV_ONES
    q_heads = [head_rows(q_t, h, DSA_DH) for h in range(DSA_HEADS)]
    _flash_init(m_ref, acc_ref)

    def scores(kb, h):
        return _dot(k_ref[0, kslice(kb), (h // per) * LANES:(h // per + 1) * LANES], q_heads[h])

    def update(kb, h, s):
        ks = kslice(kb)
        cap = pltpu.bitcast(key_ref[ks, :], F32)
        _flash_update(h, jnp.minimum(s, cap), vt_ref[0, h * dve:(h + 1) * dve, ks], m_ref, acc_ref)

    _key_block_loops(0, nkb, DSA_HEADS, scores, update)
    _flash_finish(o_ref, acc_ref, DSA_HEADS, DSA_DH)


def _dsa(q, qi, wi, k, vt, ki, b, l, tq, tk):
    nq = l // tq
    topk = min(TOPK_MAX, l // 4)
    hd = DSA_HEADS * DSA_DH
    hde = DSA_HEADS * (DSA_DH + DV_ONES)
    flat = lambda bi, i: (bi * nq + i, 0)
    batch = lambda bi, i: (bi, 0, 0)
    return pl.pallas_call(
        functools.partial(_dsa_kernel, topk=topk, tk=tk, seq=l),
        grid=(b, nq),
        in_specs=[pl.BlockSpec((tq, hd), flat),
                  pl.BlockSpec((tq, hd), flat),
                  pl.BlockSpec((tq, LANES), flat),
                  pl.BlockSpec((1, l, hd), batch, pipeline_mode=pl.Buffered(1)),
                  pl.BlockSpec((1, hde, l), batch, pipeline_mode=pl.Buffered(1)),
                  pl.BlockSpec((1, l, LANES), batch, pipeline_mode=pl.Buffered(1))],
        out_specs=pl.BlockSpec((tq, hd), flat),
        out_shape=jax.ShapeDtypeStruct((b * l, hd), BF16),
        scratch_shapes=[pltpu.VMEM((l, tq), jnp.int32), pltpu.VMEM((DSA_HEADS, tq), F32),
                        pltpu.VMEM((hde, tq), F32)],
        compiler_params=_cparams(("parallel", "arbitrary")),
    )(q, qi, wi, k, vt, ki)


def _s5_kernel(u_ref, m1_ref, m2_ref, m3_ref, pa_ref, pb_ref, y_ref):
    tc = S5_TC
    rows = u_ref.shape[0] // tc
    u = jnp.concatenate([u_ref[pl.ds(s, rows, stride=tc), :] for s in range(tc)], axis=1).astype(BF16)
    x = _dot(u, m2_ref[0])
    pos = lax.broadcasted_iota(jnp.int32, x.shape, 0)
    half = x.shape[1] // 2
    d, k = 1, 0
    while d < rows:
        sh = jnp.where(pos >= d, pltpu.roll(x, d, axis=0), 0.0)
        x = x + sh * pa_ref[0, k:k + 1, :] + pltpu.roll(sh, half, axis=1) * pb_ref[0, k:k + 1, :]
        d, k = 2 * d, k + 1
    x_in = jnp.where(pos >= 1, pltpu.roll(x, 1, axis=0), 0.0)
    y = _dot(u, m1_ref[0]) + _dot(x_in.astype(BF16), m3_ref[0])
    for t in range(tc):
        y_ref[pl.ds(t, rows, stride=tc), :] = y[:, t * LANES:(t + 1) * LANES]


def _s5(u, m1, m2, m3, pa, pb, b, l):
    nblk = S5_W // LANES
    per_block = lambda a: pl.BlockSpec((1,) + a.shape[1:], lambda ci, bi: (ci,) + (0,) * (a.ndim - 1))
    return pl.pallas_call(
        _s5_kernel,
        grid=(nblk, b),
        in_specs=[pl.BlockSpec((l, LANES), lambda ci, bi: (bi, ci))] + [per_block(a) for a in (m1, m2, m3, pa, pb)],
        out_specs=pl.BlockSpec((l, LANES), lambda ci, bi: (bi, ci)),
        out_shape=jax.ShapeDtypeStruct((b * l, S5_W), F32),
        compiler_params=_cparams(("parallel", "parallel")),
    )(u, m1, m2, m3, pa, pb)


def _s5_out_kernel(y_ref, u_ref, d_ref, gw_ref, gb_ref, o_ref):
    y = jax.nn.gelu(y_ref[...] + d_ref[...] * u_ref[...])
    gate = jax.nn.sigmoid(_dot(y.astype(BF16), gw_ref[...]) + gb_ref[...])
    o_ref[...] = (y * gate).astype(o_ref.dtype)


def _s5_out(y, u, d, gw, gb, tm=1024):
    t = y.shape[0]
    tm = min(tm, t)
    return pl.pallas_call(
        _s5_out_kernel,
        grid=(t // tm,),
        in_specs=[pl.BlockSpec((tm, S5_W), lambda i: (i, 0)),
                  pl.BlockSpec((tm, S5_W), lambda i: (i, 0)),
                  _whole(d, 1), _whole(gw, 1), _whole(gb, 1)],
        out_specs=pl.BlockSpec((tm, S5_W), lambda i: (i, 0)),
        out_shape=jax.ShapeDtypeStruct((t, S5_W), BF16),
        compiler_params=_cparams(("parallel",)),
    )(y, u, d, gw, gb)


def _pad_cols(w, n):
    return jnp.pad(w, ((0, 0), (0, n - w.shape[1])))


def _l0_weights(w_in, w_uq, w_ukv):
    q, k, v, g_lr, r, c_q, c_kv, k_rope = jnp.split(
        w_in, [256, 512, 1024, 1040, 1552, 1808, 1936], axis=1)
    half = MLA_ROPE // 2
    t1, t2 = k_rope[:, :half], k_rope[:, half:]
    z = lambda n: jnp.zeros((w_in.shape[0], n), w_in.dtype)
    tail = LANES - MLA_NOPE - MLA_ROPE
    rope_a = jnp.concatenate([z(MLA_NOPE), t1, t2, z(tail)], axis=1)
    rope_b = jnp.concatenate([z(MLA_NOPE), -t2, t1, z(tail)], axis=1)
    w0 = jnp.concatenate([q, k, v, r, c_q, c_kv, _pad_cols(g_lr, LANES), rope_a, rope_b], axis=1)
    uq = w_uq.reshape(MLA_Q_RANK, MLA_HEADS, MLA_NOPE + MLA_ROPE)
    nope, r1, r2 = uq[..., :MLA_NOPE], uq[..., MLA_NOPE:MLA_NOPE + half], uq[..., MLA_NOPE + half:]
    zq = lambda n: jnp.zeros((MLA_Q_RANK, MLA_HEADS, n), w_uq.dtype)
    wqa = jnp.concatenate([nope, r1, r2, zq(tail)], axis=-1).reshape(MLA_Q_RANK, MLA_HEADS * LANES)
    wqb = jnp.concatenate([zq(MLA_NOPE), -r2, r1, zq(tail)], axis=-1).reshape(MLA_Q_RANK, MLA_HEADS * LANES)
    ukv = w_ukv.reshape(MLA_KV_RANK, MLA_HEADS, MLA_NOPE + MLA_V)
    wk = jnp.pad(ukv[..., :MLA_NOPE], ((0, 0), (0, 0), (0, LANES - MLA_NOPE))).reshape(MLA_KV_RANK, MLA_HEADS * LANES)
    wv = ukv[..., MLA_NOPE:].reshape(MLA_KV_RANK, MLA_HEADS * MLA_V)
    return w0.astype(BF16), wqa.astype(BF16), wqb.astype(BF16), wk.astype(BF16), wv.astype(BF16)


def _l1_weights(w_in):
    q, k, v, qi, ki, wi, u = jnp.split(w_in, [512, 1024, 1536, 2048, 2112, 2120], axis=1)
    wb = jnp.concatenate([q, qi, k, _pad_cols(ki, LANES), v], axis=1)
    wf = jnp.concatenate([u, _pad_cols(wi, LANES)], axis=1)
    return wb.astype(BF16), wf.astype(BF16)


def _s5_weights(a_re, a_im, b_re, b_im, c_re, c_im, log_step, nsteps):
    lam_re = jnp.minimum(a_re, -1e-4)
    lam_im = a_im
    dt = jnp.exp(log_step)[:, None]
    mag = jnp.exp(lam_re * dt)
    abar_re = mag * jnp.cos(lam_im * dt)
    abar_im = mag * jnp.sin(lam_im * dt)
    den = jnp.square(lam_re) + jnp.square(lam_im)
    nr = abar_re - 1.0
    ni = abar_im
    coef_re = (nr * lam_re + ni * lam_im) / den
    coef_im = (ni * lam_re - nr * lam_im) / den
    bbar_re = coef_re[..., None] * b_re - coef_im[..., None] * b_im
    bbar_im = coef_re[..., None] * b_im + coef_im[..., None] * b_re
    tc, gb, p, hc = S5_TC, S5_GB, S5_STATE, S5_CH
    nblk = S5_GROUPS // gb
    pr, pi = [jnp.ones_like(abar_re)], [jnp.zeros_like(abar_im)]
    for _ in range(tc):
        pr, pi = (pr + [pr[-1] * abar_re - pi[-1] * abar_im], pi + [pr[-1] * abar_im + pi[-1] * abar_re])
    pr, pi = jnp.stack(pr), jnp.stack(pi)
    ab_re = pr[:tc, :, :, None] * bbar_re - pi[:tc, :, :, None] * bbar_im
    ab_im = pr[:tc, :, :, None] * bbar_im + pi[:tc, :, :, None] * bbar_re
    blocks = lambda a: a.reshape(a.shape[0], nblk, gb, *a.shape[2:])

    def expand(a, rw):
        y, z = a.shape[-2:]
        a = a.reshape(nblk, -1, y * z)
        src = jnp.arange(y * z)[:, None]
        dst = jnp.arange(y * gb * z)[None, :]
        spread = ((dst // (gb * z) == src // z) & (dst % z == src % z)).astype(F32)
        wide = jnp.einsum('brc,cd->brd', a, spread, precision=lax.Precision.HIGHEST)
        row_g = (jnp.arange(a.shape[1])[:, None] // rw) % gb
        return jnp.where(row_g == (dst // z) % gb, wide, 0.0)
    taps = jnp.einsum('gop,dgpi->dgoi', c_re, ab_re) - jnp.einsum('gop,dgpi->dgoi', c_im, ab_im)
    lag = jnp.arange(tc)[None, :] - jnp.arange(tc)[:, None]
    k_st = jnp.where((lag >= 0)[:, :, None, None, None], taps[jnp.clip(lag, 0, tc - 1)], 0.0)
    k_st = k_st.reshape(tc, tc, nblk, gb, hc, hc).transpose(2, 0, 3, 5, 1, 4)
    m1 = expand(k_st, hc)
    ab = jnp.stack([blocks(ab_re[::-1]), blocks(ab_im[::-1])])
    m2 = expand(ab.transpose(2, 1, 3, 5, 0, 4), hc)
    d_re = c_re[None] * pr[1:, :, None, :] - c_im[None] * pi[1:, :, None, :]
    d_im = c_re[None] * pi[1:, :, None, :] + c_im[None] * pr[1:, :, None, :]
    dd = jnp.stack([blocks(d_re), -blocks(d_im)])
    m3 = expand(dd.transpose(2, 0, 3, 5, 1, 4), p)
    qr, qi, pa, pb = pr[tc].reshape(nblk, gb * p), pi[tc].reshape(nblk, gb * p), [], []
    for _ in range(nsteps):
        pa.append(jnp.concatenate([qr, qr], axis=-1))
        pb.append(jnp.concatenate([-qi, qi], axis=-1))
        qr, qi = qr * qr - qi * qi, 2.0 * qr * qi
    return m1.astype(BF16), m2.astype(BF16), m3.astype(BF16), jnp.stack(pa, axis=1), jnp.stack(pb, axis=1)


def kernel(x, positions, l0_w_in, l0_gla_wg2, l0_gla_bg, l0_gla_norm, l0_mla_q_norm, l0_mla_w_uq, l0_mla_kv_norm, l0_mla_w_ukv, l0_w_out, l0_ln1_g, l0_ln1_b, l0_mlp_w1, l0_mlp_w2, l0_ln2_g, l0_ln2_b, l1_w_in, l1_s5_a_re, l1_s5_a_im, l1_s5_b_re, l1_s5_b_im, l1_s5_c_re, l1_s5_c_im, l1_s5_d, l1_s5_log_step, l1_glu_w, l1_glu_b, l1_w_out, l1_ln1_g, l1_ln1_b, l1_mlp_w1, l1_mlp_w2, l1_ln2_g, l1_ln2_b):
    b, l, _ = x.shape
    t = b * l
    row = lambda a: a.reshape(1, -1)
    x2 = x.reshape(t, D_MODEL)
    tq = min(256, l)
    tk = min(512, l)

    w0, wqa, wqb, wk, wv = _l0_weights(l0_w_in, l0_mla_w_uq, l0_mla_w_ukv)
    h0 = _proj(x2, w0, F32)
    o_gla = _gla(h0, l0_gla_wg2.astype(BF16), row(l0_gla_bg), row(l0_gla_norm), b, l)
    tm = min(512, l)
    half = MLA_ROPE // 2
    inv_freq = ROPE_THETA ** (-jnp.arange(half, dtype=F32) / half)
    invf = jnp.concatenate([inv_freq, inv_freq]).reshape(MLA_ROPE, 1)
    pos3 = positions.reshape(t // tm, 1, tm)
    qt, km, vt = _mla_proj(h0, pos3, invf, row(l0_mla_q_norm), row(l0_mla_kv_norm), wqa, wqb, wk, wv, b, l, tm)
    o_mla = _mla_attn(qt, km, vt, b, l, tq, tk)
    x2 = _out_ln(o_gla, o_mla, x2, l0_w_out.astype(BF16), row(l0_ln1_g), row(l0_ln1_b))
    x2 = _mlp_ln(x2, l0_mlp_w1.astype(BF16), l0_mlp_w2.astype(BF16), row(l0_ln2_g), row(l0_ln2_b))

    w1b, w1f = _l1_weights(l1_w_in)
    q1, qi1, k1, ki1, vt1, u1, wi1 = _proj1(x2, w1b, w1f, b, l)
    o_dsa = _dsa(q1, qi1, wi1, k1, vt1, ki1, b, l, tq, tk)
    seg = l // S5_TC
    s5w = _s5_weights(l1_s5_a_re, l1_s5_a_im, l1_s5_b_re, l1_s5_b_im, l1_s5_c_re, l1_s5_c_im, l1_s5_log_step,
                      max(1, (seg - 1).bit_length()))
    y1 = _s5(u1, *s5w, b, l)
    o_s5 = _s5_out(y1, u1, row(l1_s5_d), l1_glu_w.astype(BF16), row(l1_glu_b))
    x2 = _out_ln(o_dsa, o_s5, x2, l1_w_out.astype(BF16), row(l1_ln1_g), row(l1_ln1_b))
    x2 = _mlp_ln(x2, l1_mlp_w1.astype(BF16), l1_mlp_w2.astype(BF16), row(l1_ln2_g), row(l1_ln2_b))
    return x2.reshape(b, l, D_MODEL)
```
